```python
import jax, jax.numpy as jnp
from jax import lax
import numpy as np

D_MODEL = 1024
BATCH = 2
SEQ = 8192
DEPTH = 1

CHUNK = 64
Q_BLOCK = 128
N_META = 16
SB_HEADS = 8
SB_HEAD_DIM = 64
GLA_HEADS = 4
GLA_HEAD_DK = 128
GLA_HEAD_DV = 256
GATE_RANK = 16
GATE_TAU = 16.0
D_FF = 2816
CONV_W = 3
N_BRANCHES = 2
EPS = 1e-6

SB_WIDTH = SB_HEADS * SB_HEAD_DIM
GLA_K_WIDTH = GLA_HEADS * GLA_HEAD_DK
GLA_V_WIDTH = GLA_HEADS * GLA_HEAD_DV
IN_WIDTHS = (SB_WIDTH, SB_WIDTH, SB_WIDTH, GLA_K_WIDTH, GLA_K_WIDTH, GLA_V_WIDTH, GLA_V_WIDTH, GATE_RANK, N_BRANCHES * D_MODEL)
IN_TOTAL = 3 * SB_WIDTH + 2 * GLA_K_WIDTH + 2 * GLA_V_WIDTH + GATE_RANK + N_BRANCHES * D_MODEL

kernel_name = "hybrid_stickbreak_gla_convffn_block"


def _rmsnorm(x, g):
    xf = x.astype(jnp.float32)
    y = xf * lax.rsqrt(jnp.mean(xf * xf, axis=-1, keepdims=True) + EPS)
    return (y * g.astype(jnp.float32)).astype(x.dtype)


def _split_cols(a, widths):
    outs, start = [], 0
    for w in widths:
        outs.append(a[..., start:start + w])
        start += w
    return outs


def _heads(a, n_heads):
    b, l, _ = a.shape
    return a.reshape(b, l, n_heads, -1).transpose(0, 2, 1, 3)


def _stick_breaking(q, k, v):
    b, h, lp, dh = q.shape
    nb = lp // Q_BLOCK
    scale = dh ** -0.5
    kpos = jnp.arange(lp)
    qb = q.reshape(b, h, nb, Q_BLOCK, dh).transpose(2, 0, 1, 3, 4)

    def block(args):
        qi, i = args
        z = jnp.einsum('bhqd,bhkd->bhqk', qi, k).astype(jnp.float32) * scale
        qpos = i * Q_BLOCK + jnp.arange(Q_BLOCK)
        strict = kpos[None, :] < qpos[:, None]
        u = jnp.where(strict, jax.nn.log_sigmoid(-z), 0.0)
        tail = lax.cumsum(u, axis=3, reverse=True) - u
        w = jnp.where(strict, jnp.exp(jax.nn.log_sigmoid(z) + tail), 0.0)
        return jnp.einsum('bhqk,bhkd->bhqd', w.astype(v.dtype), v)

    out = lax.map(block, (qb, jnp.arange(nb)))
    return out.transpose(1, 2, 0, 3, 4).reshape(b, h, lp, dh)


def _gla(q, k, v, g):
    b, h, lp, dk = q.shape
    dv = v.shape[-1]
    nc = lp // CHUNK

    def to_chunks(a):
        return a.reshape(b, h, nc, CHUNK, a.shape[-1]).transpose(2, 0, 1, 3, 4)

    qc, kc, vc, gc = to_chunks(q), to_chunks(k), to_chunks(v), to_chunks(g)
    causal = jnp.tril(jnp.ones((CHUNK, CHUNK), dtype=bool))

    def step(state, inp):
        qi, ki, vi, gi = inp
        cum = jnp.cumsum(gi, axis=2)
        diff = cum[:, :, :, None, :] - cum[:, :, None, :, :]
        decay = jnp.exp(jnp.where(causal[None, None, :, :, None], diff, -jnp.inf))
        att = jnp.einsum('bhtd,bhsd,bhtsd->bhts', qi, ki, decay)
        o = jnp.einsum('bhts,bhsv->bhtv', att, vi) + jnp.einsum('bhtd,bhdv->bhtv', qi * jnp.exp(cum), state)
        last = cum[:, :, -1:, :]
        state = jnp.exp(last[:, :, 0, :])[..., None] * state + jnp.einsum('bhsd,bhsv->bhdv', ki * jnp.exp(last - cum), vi)
        return state, o

    state0 = jnp.zeros((b, h, dk, dv), jnp.float32)
    _, oc = lax.scan(step, state0, (qc, kc, vc, gc))
    return oc.transpose(1, 2, 0, 3, 4).reshape(b, h, lp, dv)


def _causal_dwconv(a, w, bias):
    a_p = jnp.pad(a, ((0, 0), (CONV_W - 1, 0), (0, 0)))
    y = lax.conv_general_dilated(a_p, w[:, None, :].astype(a.dtype), window_strides=(1,), padding='VALID',
                                 dimension_numbers=('NWC', 'WIO', 'NWC'), feature_group_count=a.shape[-1])
    return y + bias.astype(a.dtype)


def setup_inputs(seed: int = 0) -> dict:
    key = jax.random.key(seed)
    ks = jax.random.split(key, 20)
    f32 = jnp.float32

    def nrm(k, shape, scale):
        return jax.random.normal(k, shape, f32) * scale

    def gain(k, n):
        return 1.0 + 0.02 * jax.random.normal(k, (DEPTH, n), f32)

    return {
        "x": nrm(ks[0], (BATCH, SEQ, D_MODEL), 1.0),
        "meta_tokens": nrm(ks[1], (N_META, D_MODEL), 1.0),
        "norm_mix_pre": gain(ks[2], D_MODEL),
        "w_in": nrm(ks[3], (DEPTH, D_MODEL, IN_TOTAL), D_MODEL ** -0.5),
        "w_gk_up": nrm(ks[4], (DEPTH, GATE_RANK, GLA_K_WIDTH), GATE_RANK ** -0.5),
        "b_gk": nrm(ks[5], (DEPTH, GLA_K_WIDTH), 0.1),
        "gla_head_norm": gain(ks[6], GLA_HEAD_DV),
        "w_sb_out": nrm(ks[7], (DEPTH, SB_WIDTH, D_MODEL), SB_WIDTH ** -0.5),
        "w_gla_out": nrm(ks[8], (DEPTH, GLA_V_WIDTH, D_MODEL), GLA_V_WIDTH ** -0.5),
        "w_o": nrm(ks[9], (DEPTH, D_MODEL, D_MODEL), D_MODEL ** -0.5),
        "norm_mix_post": gain(ks[10], D_MODEL),
        "norm_ffn_pre": gain(ks[11], D_MODEL),
        "w_ffn_up": nrm(ks[12], (DEPTH, D_MODEL, D_FF), D_MODEL ** -0.5),
        "w_ffn_gate": nrm(ks[13], (DEPTH, D_MODEL, D_FF), D_MODEL ** -0.5),
        "conv_w": nrm(ks[14], (DEPTH, CONV_W, D_FF), CONV_W ** -0.5),
        "conv_b": nrm(ks[15], (DEPTH, D_FF), 0.02),
        "w_ffn_down": nrm(ks[16], (DEPTH, D_FF, D_MODEL), D_FF ** -0.5),
        "norm_ffn_post": gain(ks[17], D_MODEL),
    }


def reference(x, meta_tokens, norm_mix_pre, w_in, w_gk_up, b_gk, gla_head_norm, w_sb_out, w_gla_out, w_o,
              norm_mix_post, norm_ffn_pre, w_ffn_up, w_ffn_gate, conv_w, conv_b, w_ffn_down, norm_ffn_post):
    b, s, _ = x.shape
    L = s + N_META
    Lp = -(-L // Q_BLOCK) * Q_BLOCK
    meta = jnp.broadcast_to(meta_tokens[None].astype(x.dtype), (b, N_META, D_MODEL))
    h = jnp.concatenate([meta, x], axis=1)

    for l in range(DEPTH):
        hn = _rmsnorm(h, norm_mix_pre[l])
        hn_p = jnp.pad(hn, ((0, 0), (0, Lp - L), (0, 0)))
        proj = hn_p @ w_in[l]
        sb_q, sb_k, sb_v, g_q, g_k, g_v, g_r, g_lr, merge = _split_cols(proj, IN_WIDTHS)

        sb = _stick_breaking(_heads(sb_q, SB_HEADS), _heads(sb_k, SB_HEADS), _heads(sb_v, SB_HEADS))
        sb = sb.transpose(0, 2, 1, 3).reshape(b, Lp, SB_WIDTH)[:, :L]

        glog = jax.nn.log_sigmoid((g_lr @ w_gk_up[l] + b_gk[l]).astype(jnp.float32)) / GATE_TAU
        gq = _heads(g_q, GLA_HEADS).astype(jnp.float32) * (GLA_HEAD_DK ** -0.5)
        gk = _heads(g_k, GLA_HEADS).astype(jnp.float32)
        gv = _heads(g_v, GLA_HEADS).astype(jnp.float32)
        o = _gla(gq, gk, gv, _heads(glog, GLA_HEADS))
        o = _rmsnorm(o, gla_head_norm[l])
        o = o.transpose(0, 2, 1, 3).reshape(b, Lp, GLA_V_WIDTH)[:, :L].astype(h.dtype)
        o = o * jax.nn.silu(g_r[:, :L])

        gates = jax.nn.sigmoid(merge[:, :L])
        gate_sb, gate_gla = gates[..., :D_MODEL], gates[..., D_MODEL:]
        mix = (gate_sb * (sb @ w_sb_out[l]) + gate_gla * (o @ w_gla_out[l])) @ w_o[l]
        h = h + _rmsnorm(mix, norm_mix_post[l])

        hn = _rmsnorm(h, norm_ffn_pre[l])
        up = hn @ w_ffn_up[l]
        act = jax.nn.gelu(_causal_dwconv(up, conv_w[l], conv_b[l]), approximate=True)
        ffn = (act * (hn @ w_ffn_gate[l])) @ w_ffn_down[l]
        h = h + _rmsnorm(ffn, norm_ffn_post[l])

    return h[:, N_META:]
```

```python
import functools

import numpy as np
import jax
import jax.numpy as jnp
from jax import lax
from jax.experimental import pallas as pl
from jax.experimental.pallas import tpu as pltpu

F32 = jnp.float32
BF16 = jnp.bfloat16

D_MODEL = 1024
N_META = 16
Q_BLOCK = 128
SB_HEADS = 8
SB_HEAD_DIM = 64
GLA_HEADS = 4
GLA_HEAD_DK = 128
GLA_HEAD_DV = 256
GATE_RANK = 16
GATE_TAU = 16.0
D_FF = 2816
CONV_W = 3
EPS = 1e-6

SB_WIDTH = SB_HEADS * SB_HEAD_DIM
GLA_K_WIDTH = GLA_HEADS * GLA_HEAD_DK
GLA_V_WIDTH = GLA_HEADS * GLA_HEAD_DV

LANES = 128
MXU_WIDTH = 256
VMEM_LIMIT_BYTES = 56 * 1024 * 1024

COL_MERGE = 0
COL_SBQ = COL_MERGE + 2 * D_MODEL
COL_SBK = COL_SBQ + SB_WIDTH
COL_SBV = COL_SBK + SB_WIDTH
COL_GQ = COL_SBV + SB_WIDTH
COL_GK = COL_GQ + GLA_K_WIDTH
COL_GV = COL_GK + GLA_K_WIDTH
COL_GR = COL_GV + GLA_V_WIDTH
COL_LR = COL_GR + GLA_V_WIDTH
LR_PAD = MXU_WIDTH
PROJ_WIDTH = COL_LR + LR_PAD

ROW_TILE = 640
SB_TQ = 640
SB_TK = 128
GLA_CHUNK = 128
GLA_LEVELS = 7
FF_CHUNK = 256
SB_LOG_UNDERFLOW = -88.0


def _dot(a, b):
    return jnp.dot(a, b, preferred_element_type=F32)


def _dot_nt(a, b):
    return lax.dot_general(a, b, (((1,), (1,)), ((), ())), preferred_element_type=F32)


def _rms(x, gain):
    ms = jnp.mean(x * x, axis=-1, keepdims=True)
    return x * lax.rsqrt(ms + EPS) * gain


def _softplus(z):
    return jnp.maximum(z, 0.0) + jnp.log(1.0 + jnp.exp(-jnp.abs(z)))


def _split_hi_lo(x):
    hi = x.astype(BF16)
    lo = (x - hi.astype(F32)).astype(BF16)
    return hi, lo


def _resident(shape):
    nd = len(shape)
    return pl.BlockSpec(shape, lambda *_: (0,) * nd, pipeline_mode=pl.Buffered(1))


def _inproj_kernel(x_ref, g_ref, w_ref, o_ref):
    hn = _rms(x_ref[...], g_ref[...]).astype(BF16)
    for c in range(PROJ_WIDTH // MXU_WIDTH):
        cols = slice(c * MXU_WIDTH, (c + 1) * MXU_WIDTH)
        o_ref[:, cols] = _dot(hn, w_ref[:, cols]).astype(BF16)


def _inproj(h2d, gain, w, tm):
    m = h2d.shape[0]
    return pl.pallas_call(
        _inproj_kernel,
        grid=(m // tm,),
        in_specs=[
            pl.BlockSpec((tm, D_MODEL), lambda i: (i, 0)),
            _resident((1, D_MODEL)),
            _resident((D_MODEL, PROJ_WIDTH)),
        ],
        out_specs=pl.BlockSpec((tm, PROJ_WIDTH), lambda i: (i, 0)),
        out_shape=jax.ShapeDtypeStruct((m, PROJ_WIDTH), BF16),
        compiler_params=pltpu.CompilerParams(
            dimension_semantics=("arbitrary",), vmem_limit_bytes=VMEM_LIMIT_BYTES),
        name="inproj",
    )(h2d, gain, w)


def _sb_tail_matrix(tk):
    j = np.arange(tk)
    later = (j[:, None] > j[None, :]).astype(np.float32)
    half = np.concatenate([later, np.ones((tk, tk), np.float32)], axis=1)
    return jnp.asarray(np.concatenate([half, half], axis=0), dtype=BF16)


def _sb_kernel(q_ref, k_ref, v_ref, tt_ref, o_ref, acc_ref, c_ref, *, tq, tk):
    i = pl.program_id(2)
    nd = tq // tk
    q = q_ref[...] * (SB_HEAD_DIM ** -0.5)
    tt = tt_ref[...]
    head_a = lax.broadcasted_iota(jnp.int32, (tk, 2 * SB_HEAD_DIM), 1) < SB_HEAD_DIM

    acc_ref[...] = jnp.zeros_like(acc_ref)
    c_ref[...] = jnp.zeros_like(c_ref)

    def block_diag(x):
        zero = jnp.zeros_like(x)
        return jnp.concatenate([jnp.where(head_a, x, zero), jnp.where(head_a, zero, x)], axis=0)

    def step(jb, r0, masked):
        rows = tq - r0
        start = pl.multiple_of(jb * tk, tk)
        k_bd = block_diag(k_ref[pl.ds(start, tk), :])
        v_bd = block_diag(v_ref[pl.ds(start, tk), :])
        z = _dot_nt(q[r0:, :], k_bd)
        sp = _softplus(z)
        log_beta = z - sp
        u = -sp
        if masked:
            qpos = i * tq + r0 + lax.broadcasted_iota(jnp.int32, (rows, 2 * tk), 0)
            col = lax.broadcasted_iota(jnp.int32, (rows, 2 * tk), 1)
            kpos = jb * tk + jnp.where(col >= tk, col - tk, col)
            strict = kpos < qpos
            u = jnp.where(strict, u, 0.0)
        u_hi, u_lo = _split_hi_lo(u)
        ra = _dot(jnp.concatenate([u_hi[:, :tk], u_lo[:, :tk]], axis=1), tt)
        rb = _dot(jnp.concatenate([u_hi[:, tk:], u_lo[:, tk:]], axis=1), tt)
        carry = c_ref[r0:, :]
        tail = jnp.concatenate([ra[:, :tk], rb[:, :tk]], axis=1) + carry
        w = jnp.exp(log_beta + tail)
        if masked:
            w = jnp.where(strict, w, 0.0)
        c_ref[r0:, :] = carry + jnp.concatenate([ra[:, tk:], rb[:, tk:]], axis=1)
        acc_ref[r0:, :] += _dot(w.astype(BF16), v_bd)

    for d in range(nd - 1, -1, -1):
        step(i * nd + d, d * tk, True)

    def carry_max():
        m = jnp.max(c_ref[...], axis=0, keepdims=True)
        return jnp.max(m, axis=1, keepdims=True)[0, 0]

    def cond(state):
        jb, cmax = state
        return jnp.logical_and(jb >= 0, cmax > SB_LOG_UNDERFLOW)

    def body(state):
        jb, _ = state
        step(jb, 0, False)
        return jb - 1, carry_max()

    lax.while_loop(cond, body, (i * nd - 1, carry_max()))
    o_ref[...] = acc_ref[...].astype(o_ref.dtype)


def _stick_breaking(proj3, tq, tk):
    b, lp, _ = proj3.shape
    pair = 2 * SB_HEAD_DIM
    n_pairs = SB_WIDTH // pair
    kern = functools.partial(_sb_kernel, tq=tq, tk=tk)
    return pl.pallas_call(
        kern,
        grid=(b, n_pairs, lp // tq),
        in_specs=[
            pl.BlockSpec((None, tq, pair), lambda bi, hp, i: (bi, i, COL_SBQ // pair + hp)),
            pl.BlockSpec((None, lp, pair), lambda bi, hp, i: (bi, 0, COL_SBK // pair + hp)),
            pl.BlockSpec((None, lp, pair), lambda bi, hp, i: (bi, 0, COL_SBV // pair + hp)),
            _resident((2 * tk, 2 * tk)),
        ],
        out_specs=pl.BlockSpec((None, tq, pair), lambda bi, hp, i: (bi, i, hp)),
        out_shape=jax.ShapeDtypeStruct((b, lp, SB_WIDTH), BF16),
        scratch_shapes=[
            pltpu.VMEM((tq, pair), F32),
            pltpu.VMEM((tq, 2 * tk), F32),
        ],
        compiler_params=pltpu.CompilerParams(
            dimension_semantics=("arbitrary", "arbitrary", "arbitrary"),
            vmem_limit_bytes=VMEM_LIMIT_BYTES),
        name="stick_breaking",
    )(proj3, proj3, proj3, _sb_tail_matrix(tk))


def _gla_constants(c, levels):
    t = np.arange(c)[:, None]
    j = np.arange(c)[None, :]
    blocks = [(j <= t)]
    for l in range(levels):
        half = 1 << l
        p = (t >> (l + 1) << (l + 1)) + half
        upper = ((t >> l) & 1) == 1
        blocks.append(np.where(upper, (j >= p) & (j <= t), (j > t) & (j < p)))
    e = np.concatenate(blocks, axis=0).astype(np.float32)
    e2 = np.concatenate([e, e], axis=1)
    s = np.arange(c)[None, :]
    x = t ^ s
    lvl = np.where(x > 0, np.floor(np.log2(np.maximum(x, 1))).astype(np.int32), levels)
    lvl = np.where(s > t, -1, lvl).astype(np.int32)
    return jnp.asarray(e2, dtype=BF16), jnp.asarray(lvl)


def _gla_kernel(q_ref, k_ref, v_ref, r_ref, lr_ref, wup_ref, bgk_ref, gn_ref, e2_ref, lvl_ref,
                o_ref, st_ref, *, c, levels):
    @pl.when(pl.program_id(2) == 0)
    def _():
        st_ref[...] = jnp.zeros_like(st_ref)

    pre = _dot(lr_ref[...], wup_ref[...]) + bgk_ref[...]
    g = -_softplus(-pre) * (1.0 / GATE_TAU)
    g_hi, g_lo = _split_hi_lo(g)
    x = _dot(e2_ref[...], jnp.concatenate([g_hi, g_lo], axis=0))
    cum = x[:c]

    q = q_ref[...].astype(F32) * (GLA_HEAD_DK ** -0.5)
    k = k_ref[...].astype(F32)
    v = v_ref[...]
    lvl = lvl_ref[...]

    att = jnp.where(lvl == levels, _dot_nt(q.astype(BF16), k.astype(BF16)), 0.0)
    for l in range(levels):
        ex = jnp.exp(x[(l + 1) * c:(l + 2) * c])
        p = _dot_nt((q * ex).astype(BF16), (k * ex).astype(BF16))
        att = jnp.where(lvl == l, p, att)

    st = st_ref[...]
    o = _dot(att.astype(BF16), v) + _dot_nt((q * jnp.exp(cum)).astype(BF16), st.astype(BF16))

    last = cum[c - 1:c, :]
    k_dec = (k * jnp.exp(last - cum)).astype(BF16)
    st_ref[...] = st * jnp.exp(last) + lax.dot_general(
        v, k_dec, (((0,), (0,)), ((), ())), preferred_element_type=F32)

    r = r_ref[...].astype(F32)
    o_ref[...] = (_rms(o, gn_ref[...]) * (r * jax.nn.sigmoid(r))).astype(o_ref.dtype)


def _gla(proj3, w_up_pad, b_gk, head_gain):
    b, lp, _ = proj3.shape
    c, levels = GLA_CHUNK, GLA_LEVELS
    e2, lvl = _gla_constants(c, levels)
    dk, dv = GLA_HEAD_DK, GLA_HEAD_DV
    kern = functools.partial(_gla_kernel, c=c, levels=levels)
    return pl.pallas_call(
        kern,
        grid=(b, GLA_HEADS, lp // c),
        in_specs=[
            pl.BlockSpec((None, c, dk), lambda bi, h, t: (bi, t, COL_GQ // dk + h)),
            pl.BlockSpec((None, c, dk), lambda bi, h, t: (bi, t, COL_GK // dk + h)),
            pl.BlockSpec((None, c, dv), lambda bi, h, t: (bi, t, COL_GV // dv + h)),
            pl.BlockSpec((None, c, dv), lambda bi, h, t: (bi, t, COL_GR // dv + h)),
            pl.BlockSpec((None, c, LANES), lambda bi, h, t: (bi, t, COL_LR // LANES)),
            pl.BlockSpec((LANES, dk), lambda bi, h, t: (0, h)),
            pl.BlockSpec((1, dk), lambda bi, h, t: (0, h)),
            _resident((1, dv)),
            _resident(((levels + 1) * c, 2 * c)),
            _resident((c, c)),
        ],
        out_specs=pl.BlockSpec((None, c, dv), lambda bi, h, t: (bi, t, h)),
        out_shape=jax.ShapeDtypeStruct((b, lp, GLA_V_WIDTH), BF16),
        scratch_shapes=[pltpu.VMEM((dv, dk), F32)],
        compiler_params=pltpu.CompilerParams(
            dimension_semantics=("arbitrary", "arbitrary", "arbitrary"),
            vmem_limit_bytes=VMEM_LIMIT_BYTES),
        name="gla",
    )(proj3, proj3, proj3, proj3, proj3, w_up_pad, b_gk, head_gain, e2, lvl)


def _merge_kernel(h_ref, m_ref, sb_ref, og_ref, wsb_ref, wgla_ref, wo_ref, gn_ref, o_ref):
    a = _dot(sb_ref[...], wsb_ref[...])
    b = _dot(og_ref[...], wgla_ref[...])
    m = m_ref[...].astype(F32)
    mixed = jax.nn.sigmoid(m[:, :D_MODEL]) * a + jax.nn.sigmoid(m[:, D_MODEL:]) * b
    mix = _dot(mixed.astype(BF16), wo_ref[...])
    o_ref[...] = h_ref[...] + _rms(mix, gn_ref[...])


def _merge(h2d, proj2d, sb2d, og2d, w_sb, w_gla, w_o, gain, tm):
    m = h2d.shape[0]
    return pl.pallas_call(
        _merge_kernel,
        grid=(m // tm,),
        in_specs=[
            pl.BlockSpec((tm, D_MODEL), lambda i: (i, 0)),
            pl.BlockSpec((tm, 2 * D_MODEL), lambda i: (i, COL_MERGE // (2 * D_MODEL))),
            pl.BlockSpec((tm, SB_WIDTH), lambda i: (i, 0)),
            pl.BlockSpec((tm, GLA_V_WIDTH), lambda i: (i, 0)),
            _resident((SB_WIDTH, D_MODEL)),
            _resident((GLA_V_WIDTH, D_MODEL)),
            _resident((D_MODEL, D_MODEL)),
            _resident((1, D_MODEL)),
        ],
        out_specs=pl.BlockSpec((tm, D_MODEL), lambda i: (i, 0)),
        out_shape=jax.ShapeDtypeStruct((m, D_MODEL), F32),
        compiler_params=pltpu.CompilerParams(
            dimension_semantics=("arbitrary",), vmem_limit_bytes=VMEM_LIMIT_BYTES),
        name="merge",
    )(h2d, proj2d, sb2d, og2d, w_sb, w_gla, w_o, gain)


def _ffn_kernel(h_ref, gpre_ref, wup_ref, wgate_ref, cw_ref, cb_ref, wdown_ref, gpost_ref,
                o_ref, tail_ref, acc_ref, *, tm, tiles_per_seq):
    @pl.when(pl.program_id(0) % tiles_per_seq == 0)
    def _():
        tail_ref[...] = jnp.zeros_like(tail_ref)

    h = h_ref[...]
    hn = _rms(h, gpre_ref[...]).astype(BF16)
    row = lax.broadcasted_iota(jnp.int32, (tm, FF_CHUNK), 0)
    for c in range(D_FF // FF_CHUNK):
        cols = slice(c * FF_CHUNK, (c + 1) * FF_CHUNK)
        up = _dot(hn, wup_ref[:, cols])
        gate = _dot(hn, wgate_ref[:, cols])
        prev = tail_ref[:, cols]
        up1 = jnp.where(row == 0, prev[7:8, :], pltpu.roll(up, 1, 0))
        up2 = jnp.where(row == 0, prev[6:7, :],
                        jnp.where(row == 1, prev[7:8, :], pltpu.roll(up, 2, 0)))
        tail_ref[:, cols] = up[tm - 8:, :]
        cw = cw_ref[:, cols]
        y = cw[0:1, :] * up2 + cw[1:2, :] * up1 + cw[2:3, :] * up + cb_ref[:, cols]
        act = jax.nn.gelu(y, approximate=True) * gate
        contrib = _dot(act.astype(BF16), wdown_ref[cols, :])
        if c == 0:
            acc_ref[...] = contrib
        else:
            acc_ref[...] += contrib
    o_ref[...] = h + _rms(acc_ref[...], gpost_ref[...])


def _ffn(h2d, g_pre, w_up, w_gate, conv_w, conv_b, w_down, g_post, tm, tiles_per_seq):
    m = h2d.shape[0]
    kern = functools.partial(_ffn_kernel, tm=tm, tiles_per_seq=tiles_per_seq)
    return pl.pallas_call(
        kern,
        grid=(m // tm,),
        in_specs=[
            pl.BlockSpec((tm, D_MODEL), lambda i: (i, 0)),
            _resident((1, D_MODEL)),
            _resident((D_MODEL, D_FF)),
            _resident((D_MODEL, D_FF)),
            _resident((CONV_W, D_FF)),
            _resident((1, D_FF)),
            _resident((D_FF, D_MODEL)),
            _resident((1, D_MODEL)),
        ],
        out_specs=pl.BlockSpec((tm, D_MODEL), lambda i: (i, 0)),
        out_shape=jax.ShapeDtypeStruct((m, D_MODEL), F32),
        scratch_shapes=[
            pltpu.VMEM((8, D_FF), F32),
            pltpu.VMEM((tm, D_MODEL), F32),
        ],
        compiler_params=pltpu.CompilerParams(
            dimension_semantics=("arbitrary",), vmem_limit_bytes=VMEM_LIMIT_BYTES),
        name="ffn",
    )(h2d, g_pre, w_up, w_gate, conv_w, conv_b, w_down, g_post)


def _regroup_w_in(w):
    widths = (SB_WIDTH, SB_WIDTH, SB_WIDTH, GLA_K_WIDTH, GLA_K_WIDTH, GLA_V_WIDTH, GLA_V_WIDTH,
              GATE_RANK, 2 * D_MODEL)
    starts = np.concatenate([[0], np.cumsum(widths)])
    part = [w[:, starts[n]:starts[n + 1]] for n in range(len(widths))]
    pad = jnp.zeros((w.shape[0], LR_PAD - GATE_RANK), w.dtype)
    return jnp.concatenate([part[8]] + part[0:7] + [part[7], pad], axis=1).astype(BF16)


def _layer(h, norm_mix_pre, w_in, w_gk_up, b_gk, gla_head_norm, w_sb_out, w_gla_out, w_o,
           norm_mix_post, norm_ffn_pre, w_ffn_up, w_ffn_gate, conv_w, conv_b, w_ffn_down,
           norm_ffn_post, tm, tq):
    b, lp, _ = h.shape
    h2d = h.reshape(b * lp, D_MODEL)
    proj2d = _inproj(h2d, norm_mix_pre[None, :], _regroup_w_in(w_in), tm)
    proj3 = proj2d.reshape(b, lp, PROJ_WIDTH)

    sb = _stick_breaking(proj3, tq, SB_TK)
    w_up_pad = jnp.zeros((LANES, GLA_K_WIDTH), BF16).at[:GATE_RANK].set(w_gk_up.astype(BF16))
    og = _gla(proj3, w_up_pad, b_gk[None, :], gla_head_norm[None, :])

    h2d = _merge(h2d, proj2d, sb.reshape(b * lp, SB_WIDTH), og.reshape(b * lp, GLA_V_WIDTH),
                 w_sb_out.astype(BF16), w_gla_out.astype(BF16), w_o.astype(BF16),
                 norm_mix_post[None, :], tm)
    h2d = _ffn(h2d, norm_ffn_pre[None, :], w_ffn_up.astype(BF16), w_ffn_gate.astype(BF16),
               conv_w, conv_b[None, :], w_ffn_down.astype(BF16), norm_ffn_post[None, :],
               tm, lp // tm)
    return h2d.reshape(b, lp, D_MODEL)


def _pick_tile(lp, want):
    t = want
    while lp % t:
        t -= Q_BLOCK
    return t


def kernel(x, meta_tokens, norm_mix_pre, w_in, w_gk_up, b_gk, gla_head_norm, w_sb_out, w_gla_out, w_o,
           norm_mix_post, norm_ffn_pre, w_ffn_up, w_ffn_gate, conv_w, conv_b, w_ffn_down, norm_ffn_post):
    b, s, _ = x.shape
    l = s + N_META
    lp = -(-l // Q_BLOCK) * Q_BLOCK
    tm = _pick_tile(lp, ROW_TILE)
    tq = _pick_tile(lp, SB_TQ)
    meta = jnp.broadcast_to(meta_tokens[None].astype(x.dtype), (b, N_META, D_MODEL))
    h = jnp.concatenate([meta, x, jnp.zeros((b, lp - l, D_MODEL), x.dtype)], axis=1)
    for d in range(norm_mix_pre.shape[0]):
        h = _layer(h, norm_mix_pre[d], w_in[d], w_gk_up[d], b_gk[d], gla_head_norm[d], w_sb_out[d],
                   w_gla_out[d], w_o[d], norm_mix_post[d], norm_ffn_pre[d], w_ffn_up[d],
                   w_ffn_gate[d], conv_w[d], conv_b[d], w_ffn_down[d], norm_ffn_post[d], tm, tq)
    return h[:, N_META:l]
```

```python
import functools

import numpy as np
import jax
import jax.numpy as jnp
from jax import lax
from jax.experimental import pallas as pl
from jax.experimental.pallas import tpu as pltpu

F32 = jnp.float32
BF16 = jnp.bfloat16

D_MODEL = 1024
N_META = 16
Q_BLOCK = 128
SB_HEADS = 8
SB_HEAD_DIM = 64
GLA_HEADS = 4
GLA_HEAD_DK = 128
GLA_HEAD_DV = 256
GATE_RANK = 16
GATE_TAU = 16.0
D_FF = 2816
CONV_W = 3
EPS = 1e-6

SB_WIDTH = SB_HEADS * SB_HEAD_DIM
GLA_K_WIDTH = GLA_HEADS * GLA_HEAD_DK
GLA_V_WIDTH = GLA_HEADS * GLA_HEAD_DV

LANES = 128
SUBLANES = 8
MXU_WIDTH = 256
VMEM_LIMIT_BYTES = 56 * 1024 * 1024

COL_SBQ = 0
COL_SBK = COL_SBQ + SB_WIDTH
COL_SBV = COL_SBK + SB_WIDTH
COL_GQ = COL_SBV + SB_WIDTH
COL_GK = COL_GQ + GLA_K_WIDTH
COL_GV = COL_GK + GLA_K_WIDTH
COL_GR = COL_GV + GLA_V_WIDTH
COL_LR = COL_GR + GLA_V_WIDTH
COL_MERGE = COL_LR + GATE_RANK
PROJ_WIDTH = COL_LR + LANES

FRONT = Q_BLOCK
ROW_TILE = 640
FFN_TILE = 512
SB_TQ = 640
SB_TK = 128
GLA_CHUNK = 128
GLA_LEVELS = 7
GLA_MXU_LEVELS = 3
GLA_VBLOCK = 512
FF_CHUNK = 256
SB_LOG_UNDERFLOW = -88.0


def _dot(a, b):
    return jnp.dot(a, b, preferred_element_type=F32)


def _dot_nt(a, b):
    return lax.dot_general(a, b, (((1,), (1,)), ((), ())), preferred_element_type=F32)


def _rms(x, gain):
    ms = jnp.mean(x * x, axis=-1, keepdims=True)
    return x * lax.rsqrt(ms + EPS) * gain


def _softplus(z):
    return jnp.maximum(z, 0.0) + jnp.log(1.0 + jnp.exp(-jnp.abs(z)))


def _split_hi_lo(x):
    hi = x.astype(BF16)
    lo = (x - hi.astype(F32)).astype(BF16)
    return hi, lo


def _resident(shape):
    nd = len(shape)
    return pl.BlockSpec(shape, lambda *_: (0,) * nd, pipeline_mode=pl.Buffered(1))


def _x_window(tm):
    return pl.BlockSpec(
        (pl.Squeezed(), pl.Element(tm), pl.Element(D_MODEL)),
        lambda b, i: (b, pl.multiple_of(jnp.maximum(i * tm - FRONT, 0), Q_BLOCK), 0))


def _padded_rows(x_ref, meta_ref, h_ref, tm):
    i = pl.program_id(1)

    @pl.when(i == 0)
    def _():
        h_ref[0:FRONT - N_META, :] = jnp.zeros((FRONT - N_META, D_MODEL), F32)
        h_ref[FRONT - N_META:FRONT, :] = meta_ref[...]
        h_ref[FRONT:, :] = x_ref[0:tm - FRONT, :]

    @pl.when(i > 0)
    def _():
        h_ref[...] = x_ref[...]

    return h_ref[...]


def _inproj_kernel(x_ref, meta_ref, g_ref, w_ref, o_ref, h_ref, *, tm):
    h = _padded_rows(x_ref, meta_ref, h_ref, tm)
    hn = _rms(h, g_ref[...]).astype(BF16)
    for c0 in range(0, PROJ_WIDTH, MXU_WIDTH):
        cols = slice(c0, min(c0 + MXU_WIDTH, PROJ_WIDTH))
        o_ref[:, cols] = _dot(hn, w_ref[:, cols]).astype(BF16)


def _inproj(x, meta, gain, w, tm, lp):
    b = x.shape[0]
    kern = functools.partial(_inproj_kernel, tm=tm)
    return pl.pallas_call(
        kern,
        grid=(b, lp // tm),
        in_specs=[
            _x_window(tm),
            _resident((N_META, D_MODEL)),
            _resident((1, D_MODEL)),
            _resident((D_MODEL, PROJ_WIDTH)),
        ],
        out_specs=pl.BlockSpec((None, tm, PROJ_WIDTH), lambda bi, i: (bi, i, 0)),
        out_shape=jax.ShapeDtypeStruct((b, lp, PROJ_WIDTH), BF16),
        scratch_shapes=[pltpu.VMEM((tm, D_MODEL), F32)],
        compiler_params=pltpu.CompilerParams(
            dimension_semantics=("arbitrary", "arbitrary"), vmem_limit_bytes=VMEM_LIMIT_BYTES),
        name="inproj",
    )(x, meta, gain, w)


def _sb_tail_matrix(tk):
    j = np.arange(tk)
    later = (j[:, None] > j[None, :]).astype(np.float32)
    half = np.concatenate([later, np.ones((tk, tk), np.float32)], axis=1)
    return jnp.asarray(np.concatenate([half, half], axis=0), dtype=BF16)


def _sb_kernel(q_ref, k_ref, v_ref, tt_ref, o_ref, acc_ref, c_ref, *, tq, tk):
    i = pl.program_id(2)
    nd = tq // tk
    q = q_ref[...] * (SB_HEAD_DIM ** -0.5)
    tt = tt_ref[...]
    head_a = lax.broadcasted_iota(jnp.int32, (tk, 2 * SB_HEAD_DIM), 1) < SB_HEAD_DIM

    acc_ref[...] = jnp.zeros_like(acc_ref)
    c_ref[...] = jnp.zeros_like(c_ref)

    def block_diag(x):
        zero = jnp.zeros_like(x)
        return jnp.concatenate([jnp.where(head_a, x, zero), jnp.where(head_a, zero, x)], axis=0)

    def step(jb, r0, masked):
        rows = tq - r0
        start = pl.multiple_of(jb * tk, tk)
        k_bd = block_diag(k_ref[pl.ds(start, tk), :])
        v_bd = block_diag(v_ref[pl.ds(start, tk), :])
        z = _dot_nt(q[r0:, :], k_bd)
        sp = _softplus(z)
        log_beta = z - sp
        u = -sp
        if masked:
            qpos = i * tq + r0 + lax.broadcasted_iota(jnp.int32, (rows, 2 * tk), 0)
            col = lax.broadcasted_iota(jnp.int32, (rows, 2 * tk), 1)
            kpos = jb * tk + jnp.where(col >= tk, col - tk, col)
            strict = jnp.logical_and(kpos < qpos, kpos >= FRONT - N_META)
            u = jnp.where(strict, u, 0.0)
        u_hi, u_lo = _split_hi_lo(u)
        ra = _dot(jnp.concatenate([u_hi[:, :tk], u_lo[:, :tk]], axis=1), tt)
        rb = _dot(jnp.concatenate([u_hi[:, tk:], u_lo[:, tk:]], axis=1), tt)
        carry = c_ref[r0:, :]
        tail = jnp.concatenate([ra[:, :tk], rb[:, :tk]], axis=1) + carry
        w = jnp.exp(log_beta + tail)
        if masked:
            w = jnp.where(strict, w, 0.0)
        c_ref[r0:, :] = carry + jnp.concatenate([ra[:, tk:], rb[:, tk:]], axis=1)
        acc_ref[r0:, :] += _dot(w.astype(BF16), v_bd)

    for d in range(nd - 1, -1, -1):
        step(i * nd + d, d * tk, True)

    def carry_max():
        m = jnp.max(c_ref[...], axis=0, keepdims=True)
        return jnp.max(m, axis=1, keepdims=True)[0, 0]

    def cond(state):
        jb, cmax = state
        return jnp.logical_and(jb >= 1, cmax > SB_LOG_UNDERFLOW)

    def body(state):
        jb, _ = state
        step(jb, 0, False)
        return jb - 1, carry_max()

    jb_end, cmax_end = lax.while_loop(cond, body, (i * nd - 1, carry_max()))

    @pl.when(jnp.logical_and(jb_end == 0, cmax_end > SB_LOG_UNDERFLOW))
    def _():
        step(0, 0, True)

    o_ref[...] = acc_ref[...].astype(o_ref.dtype)


def _stick_breaking(proj3, tq, tk):
    b, lp, _ = proj3.shape
    pair = 2 * SB_HEAD_DIM
    n_pairs = SB_WIDTH // pair
    kern = functools.partial(_sb_kernel, tq=tq, tk=tk)
    return pl.pallas_call(
        kern,
        grid=(b, n_pairs, lp // tq),
        in_specs=[
            pl.BlockSpec((None, tq, pair), lambda bi, hp, i: (bi, i, COL_SBQ // pair + hp)),
            pl.BlockSpec((None, lp, pair), lambda bi, hp, i: (bi, 0, COL_SBK // pair + hp)),
            pl.BlockSpec((None, lp, pair), lambda bi, hp, i: (bi, 0, COL_SBV // pair + hp)),
            _resident((2 * tk, 2 * tk)),
        ],
        out_specs=pl.BlockSpec((None, tq, pair), lambda bi, hp, i: (bi, i, hp)),
        out_shape=jax.ShapeDtypeStruct((b, lp, SB_WIDTH), BF16),
        scratch_shapes=[
            pltpu.VMEM((tq, pair), F32),
            pltpu.VMEM((tq, 2 * tk), F32),
        ],
        compiler_params=pltpu.CompilerParams(
            dimension_semantics=("arbitrary", "arbitrary", "arbitrary"),
            vmem_limit_bytes=VMEM_LIMIT_BYTES),
        name="stick_breaking",
    )(proj3, proj3, proj3, _sb_tail_matrix(tk))


def _gla_constants(c, levels, mxu_levels):
    t = np.arange(c)[:, None]
    j = np.arange(c)[None, :]
    blocks = [(j <= t)]
    for l in range(mxu_levels):
        half = 1 << l
        p = (t >> (l + 1) << (l + 1)) + half
        upper = ((t >> l) & 1) == 1
        blocks.append(np.where(upper, (j >= p) & (j <= t), (j > t) & (j < p)))
    e = np.concatenate(blocks, axis=0).astype(np.float32)
    e2 = np.concatenate([e, e], axis=1)
    s = np.arange(c)[None, :]
    x = t ^ s
    lvl = np.where(x > 0, np.floor(np.log2(np.maximum(x, 1))).astype(np.int32), levels)
    lvl = np.where(s > t, -1, lvl).astype(np.int32)
    return jnp.asarray(e2, dtype=BF16), jnp.asarray(lvl)


def _gla_kernel(q_ref, k_ref, v01_ref, v23_ref, r01_ref, r23_ref, lr_ref, wup_ref, bgk_ref,
                gn_ref, e2_ref, lvl_ref, o_ref, st_ref, *, c, levels, mxu_levels):
    dk, dv = GLA_HEAD_DK, GLA_HEAD_DV

    @pl.when(pl.program_id(1) == 0)
    def _():
        st_ref[...] = jnp.zeros_like(st_ref)

    pre = _dot(lr_ref[...], wup_ref[...]) + bgk_ref[...]
    g = -_softplus(-pre) * (1.0 / GATE_TAU)
    g_hi, g_lo = _split_hi_lo(g)
    x = _dot(e2_ref[...], jnp.concatenate([g_hi, g_lo], axis=0))
    cum = x[:c]

    def level_exponent(l):
        if l < mxu_levels:
            return x[(l + 1) * c:(l + 2) * c]
        half = 1 << l
        parts = []
        for base in range(0, c, 2 * half):
            p = base + half
            ref = jnp.broadcast_to(cum[p - 1:p, :], (half, cum.shape[1]))
            parts.append(ref - cum[base:p])
            parts.append(cum[p:p + half] - ref)
        return jnp.concatenate(parts, axis=0)

    q = q_ref[...].astype(F32) * (dk ** -0.5)
    k = k_ref[...].astype(F32)
    lvl = lvl_ref[...]
    heads = [slice(h * dk, (h + 1) * dk) for h in range(GLA_HEADS)]

    q16, k16 = q.astype(BF16), k.astype(BF16)
    att = [jnp.where(lvl == levels, _dot_nt(q16[:, hs], k16[:, hs]), 0.0) for hs in heads]
    for l in range(levels):
        ex = jnp.exp(level_exponent(l))
        ql, kl = (q * ex).astype(BF16), (k * ex).astype(BF16)
        att = [jnp.where(lvl == l, _dot_nt(ql[:, hs], kl[:, hs]), a) for hs, a in zip(heads, att)]

    last = cum[c - 1:c, :]
    q_dec = (q * jnp.exp(cum)).astype(BF16)
    k_dec = (k * jnp.exp(last - cum)).astype(BF16)
    st_decay = jnp.exp(last)
    v_refs = (v01_ref, v01_ref, v23_ref, v23_ref)
    outs = []
    for h, hs in enumerate(heads):
        v = v_refs[h][:, (h % 2) * dv:(h % 2 + 1) * dv]
        st = st_ref[h]
        o = _dot(att[h].astype(BF16), v) + _dot_nt(q_dec[:, hs], st.astype(BF16))
        st_ref[h] = st * st_decay[:, hs] + lax.dot_general(
            v, k_dec[:, hs], (((0,), (0,)), ((), ())), preferred_element_type=F32)
        outs.append(_rms(o, gn_ref[...]))
    r = jnp.concatenate([r01_ref[...], r23_ref[...]], axis=1).astype(F32)
    o_ref[...] = (jnp.concatenate(outs, axis=1) * (r * jax.nn.sigmoid(r))).astype(o_ref.dtype)


def _gla(proj3, w_up_pad, b_gk, head_gain):
    b, lp, _ = proj3.shape
    c, levels, mxu_levels = GLA_CHUNK, GLA_LEVELS, GLA_MXU_LEVELS
    e2, lvl = _gla_constants(c, levels, mxu_levels)
    kw, vw, vb = GLA_K_WIDTH, GLA_V_WIDTH, GLA_VBLOCK
    kern = functools.partial(_gla_kernel, c=c, levels=levels, mxu_levels=mxu_levels)
    return pl.pallas_call(
        kern,
        grid=(b, lp // c),
        in_specs=[
            pl.BlockSpec((None, c, kw), lambda bi, t: (bi, t, COL_GQ // kw)),
            pl.BlockSpec((None, c, kw), lambda bi, t: (bi, t, COL_GK // kw)),
            pl.BlockSpec((None, c, vb), lambda bi, t: (bi, t, COL_GV // vb)),
            pl.BlockSpec((None, c, vb), lambda bi, t: (bi, t, COL_GV // vb + 1)),
            pl.BlockSpec((None, c, vb), lambda bi, t: (bi, t, COL_GR // vb)),
            pl.BlockSpec((None, c, vb), lambda bi, t: (bi, t, COL_GR // vb + 1)),
            pl.BlockSpec((None, c, LANES), lambda bi, t: (bi, t, COL_LR // LANES)),
            _resident((LANES, kw)),
            _resident((1, kw)),
            _resident((1, GLA_HEAD_DV)),
            _resident(((mxu_levels + 1) * c, 2 * c)),
            _resident((c, c)),
        ],
        out_specs=pl.BlockSpec((None, c, vw), lambda bi, t: (bi, t, 0)),
        out_shape=jax.ShapeDtypeStruct((b, lp, vw), BF16),
        scratch_shapes=[pltpu.VMEM((GLA_HEADS, GLA_HEAD_DV, GLA_HEAD_DK), F32)],
        compiler_params=pltpu.CompilerParams(
            dimension_semantics=("arbitrary", "arbitrary"),
            vmem_limit_bytes=VMEM_LIMIT_BYTES),
        name="gla",
    )(proj3, proj3, proj3, proj3, proj3, proj3, proj3, w_up_pad, b_gk, head_gain, e2, lvl)


def _merge_kernel(x_ref, meta_ref, gpre_ref, wm_ref, sb_ref, og_ref, wsb_ref, wgla_ref, wo_ref,
                  gpost_ref, o_ref, h_ref, *, tm):
    h = _padded_rows(x_ref, meta_ref, h_ref, tm)
    hn = _rms(h, gpre_ref[...]).astype(BF16)
    m = _dot(hn, wm_ref[...])
    a = _dot(sb_ref[...], wsb_ref[...])
    b = _dot(og_ref[...], wgla_ref[...])
    mixed = jax.nn.sigmoid(m[:, :D_MODEL]) * a + jax.nn.sigmoid(m[:, D_MODEL:]) * b
    mix = _dot(mixed.astype(BF16), wo_ref[...])
    o_ref[...] = h + _rms(mix, gpost_ref[...])


def _merge(x, meta, g_pre, w_merge, sb, og, w_sb, w_gla, w_o, g_post, tm, lp):
    b = x.shape[0]
    kern = functools.partial(_merge_kernel, tm=tm)
    return pl.pallas_call(
        kern,
        grid=(b, lp // tm),
        in_specs=[
            _x_window(tm),
            _resident((N_META, D_MODEL)),
            _resident((1, D_MODEL)),
            _resident((D_MODEL, 2 * D_MODEL)),
            pl.BlockSpec((None, tm, SB_WIDTH), lambda bi, i: (bi, i, 0)),
            pl.BlockSpec((None, tm, GLA_V_WIDTH), lambda bi, i: (bi, i, 0)),
            _resident((SB_WIDTH, D_MODEL)),
            _resident((GLA_V_WIDTH, D_MODEL)),
            _resident((D_MODEL, D_MODEL)),
            _resident((1, D_MODEL)),
        ],
        out_specs=pl.BlockSpec((None, tm, D_MODEL), lambda bi, i: (bi, i, 0)),
        out_shape=jax.ShapeDtypeStruct((b, lp, D_MODEL), F32),
        scratch_shapes=[pltpu.VMEM((tm, D_MODEL), F32)],
        compiler_params=pltpu.CompilerParams(
            dimension_semantics=("arbitrary", "arbitrary"), vmem_limit_bytes=VMEM_LIMIT_BYTES),
        name="merge",
    )(x, meta, g_pre, w_merge, sb, og, w_sb, w_gla, w_o, g_post)


def _ffn_kernel(h_ref, hist_ref, gpre_ref, wup_ref, wgate_ref, cw_ref, cb_ref, wdown_ref,
                gpost_ref, o_ref, tail_ref, act_ref, *, tm):
    chunks = [slice(c0, c0 + FF_CHUNK) for c0 in range(0, D_FF, FF_CHUNK)]

    @pl.when(pl.program_id(1) == 0)
    def _():
        hist = _rms(hist_ref[...], gpre_ref[...]).astype(BF16)
        for cols in chunks:
            tail_ref[:, cols] = _dot(hist, wup_ref[:, cols])

    h = h_ref[...]
    hn = _rms(h, gpre_ref[...]).astype(BF16)
    row = lax.broadcasted_iota(jnp.int32, (SUBLANES, FF_CHUNK), 0)
    for cols in chunks:
        up = _dot(hn, wup_ref[:, cols])
        gate = _dot(hn, wgate_ref[:, cols])
        prev = tail_ref[:, cols]
        r1 = pltpu.roll(up, 1, 0)
        r2 = pltpu.roll(up, 2, 0)
        head1 = jnp.where(row == 0, prev[7:8, :], r1[:SUBLANES])
        head2 = jnp.where(row == 0, prev[6:7, :], jnp.where(row == 1, prev[7:8, :], r2[:SUBLANES]))
        up1 = jnp.concatenate([head1, r1[SUBLANES:]], axis=0)
        up2 = jnp.concatenate([head2, r2[SUBLANES:]], axis=0)
        tail_ref[:, cols] = up[tm - SUBLANES:, :]
        cw = cw_ref[:, cols]
        y = cw[0:1, :] * up2 + cw[1:2, :] * up1 + cw[2:3, :] * up + cb_ref[:, cols]
        act_ref[:, cols] = (jax.nn.gelu(y, approximate=True) * gate).astype(BF16)
    ffn = _dot(act_ref[...], wdown_ref[...])
    o_ref[...] = h + _rms(ffn, gpost_ref[...])


def _ffn(h3, g_pre, w_up, w_gate, conv_w, conv_b, w_down, g_post, tm):
    b, lp, _ = h3.shape
    s = lp - FRONT
    kern = functools.partial(_ffn_kernel, tm=tm)

    def rows(n, start):
        return pl.BlockSpec((pl.Squeezed(), pl.Element(n), pl.Element(D_MODEL)), start)

    return pl.pallas_call(
        kern,
        grid=(b, s // tm),
        in_specs=[
            rows(tm, lambda bi, i: (bi, pl.multiple_of(FRONT + i * tm, Q_BLOCK), 0)),
            rows(SUBLANES, lambda bi, i: (bi, FRONT - SUBLANES, 0)),
            _resident((1, D_MODEL)),
            _resident((D_MODEL, D_FF)),
            _resident((D_MODEL, D_FF)),
            _resident((CONV_W, D_FF)),
            _resident((1, D_FF)),
            _resident((D_FF, D_MODEL)),
            _resident((1, D_MODEL)),
        ],
        out_specs=pl.BlockSpec((None, tm, D_MODEL), lambda bi, i: (bi, i, 0)),
        out_shape=jax.ShapeDtypeStruct((b, s, D_MODEL), F32),
        scratch_shapes=[
            pltpu.VMEM((SUBLANES, D_FF), F32),
            pltpu.VMEM((tm, D_FF), BF16),
        ],
        compiler_params=pltpu.CompilerParams(
            dimension_semantics=("arbitrary", "arbitrary"), vmem_limit_bytes=VMEM_LIMIT_BYTES),
        name="ffn",
    )(h3, h3, g_pre, w_up, w_gate, conv_w, conv_b, w_down, g_post)


def _pick_tile(n, want):
    t = want
    while n % t:
        t -= Q_BLOCK
    return t


def kernel(x, meta_tokens, norm_mix_pre, w_in, w_gk_up, b_gk, gla_head_norm, w_sb_out, w_gla_out, w_o,
           norm_mix_post, norm_ffn_pre, w_ffn_up, w_ffn_gate, conv_w, conv_b, w_ffn_down, norm_ffn_post):
    b, s, _ = x.shape
    assert s % Q_BLOCK == 0 and norm_mix_pre.shape[0] == 1
    lp = FRONT + s
    tm = _pick_tile(lp, ROW_TILE)
    tq = _pick_tile(lp, SB_TQ)
    tf = _pick_tile(s, FFN_TILE)
    d = 0

    w_main = jnp.pad(w_in[d][:, :COL_MERGE].astype(BF16), ((0, 0), (0, PROJ_WIDTH - COL_MERGE)))
    w_merge = w_in[d][:, COL_MERGE:].astype(BF16)
    w_up_pad = jnp.pad(w_gk_up[d].astype(BF16), ((0, LANES - GATE_RANK), (0, 0)))

    proj3 = _inproj(x, meta_tokens, norm_mix_pre[d][None, :], w_main, tm, lp)
    sb = _stick_breaking(proj3, tq, SB_TK)
    og = _gla(proj3, w_up_pad, b_gk[d][None, :], gla_head_norm[d][None, :])
    h1 = _merge(x, meta_tokens, norm_mix_pre[d][None, :], w_merge, sb, og,
                w_sb_out[d].astype(BF16), w_gla_out[d].astype(BF16), w_o[d].astype(BF16),
                norm_mix_post[d][None, :], tm, lp)
    return _ffn(h1, norm_ffn_pre[d][None, :], w_ffn_up[d].astype(BF16), w_ffn_gate[d].astype(BF16),
                conv_w[d], conv_b[d][None, :], w_ffn_down[d].astype(BF16),
                norm_ffn_post[d][None, :], tf)
```

```python
import functools

import numpy as np
import jax
import jax.numpy as jnp
from jax import lax
from jax.experimental import pallas as pl
from jax.experimental.pallas import tpu as pltpu

F32 = jnp.float32
BF16 = jnp.bfloat16

D_MODEL = 1024
N_META = 16
Q_BLOCK = 128
SB_HEADS = 8
SB_HEAD_DIM = 64
GLA_HEADS = 4
GLA_HEAD_DK = 128
GLA_HEAD_DV = 256
GATE_RANK = 16
GATE_TAU = 16.0
D_FF = 2816
CONV_W = 3
EPS = 1e-6

SB_WIDTH = SB_HEADS * SB_HEAD_DIM
GLA_K_WIDTH = GLA_HEADS * GLA_HEAD_DK
GLA_V_WIDTH = GLA_HEADS * GLA_HEAD_DV

LANES = 128
SUBLANES = 8
MXU_WIDTH = 256
VMEM_LIMIT_BYTES = 56 * 1024 * 1024

COL_SBQ = 0
COL_SBK = COL_SBQ + SB_WIDTH
COL_SBV = COL_SBK + SB_WIDTH
COL_GQ = COL_SBV + SB_WIDTH
COL_GK = COL_GQ + GLA_K_WIDTH
COL_GV = COL_GK + GLA_K_WIDTH
COL_GR = COL_GV + GLA_V_WIDTH
COL_LR = COL_GR + GLA_V_WIDTH
COL_MERGE = COL_LR + GATE_RANK
PROJ_WIDTH = COL_LR + LANES

FRONT = Q_BLOCK
ROW_TILE = 640
FFN_TILE = 1024
SB_TQ = 640
SB_TK = 128
GLA_CHUNK = 128
GLA_LEVELS = 7
GLA_MXU_LEVELS = 3
GLA_VBLOCK = 512
FF_CHUNK = 256
LOG2_E = 1.4426950408889634
SB_LOG2_UNDERFLOW = -126.0


def _dot(a, b):
    return jnp.dot(a, b, preferred_element_type=F32)


def _dot_nt(a, b):
    return lax.dot_general(a, b, (((1,), (1,)), ((), ())), preferred_element_type=F32)


def _rms(x, gain):
    ms = jnp.mean(x * x, axis=-1, keepdims=True)
    return x * lax.rsqrt(ms + EPS) * gain


def _split_hi_lo(x):
    hi = x.astype(BF16)
    lo = (x - hi.astype(F32)).astype(BF16)
    return hi, lo


def _resident(shape):
    nd = len(shape)
    return pl.BlockSpec(shape, lambda *_: (0,) * nd, pipeline_mode=pl.Buffered(1))


def _x_window(tm):
    return pl.BlockSpec(
        (pl.Squeezed(), pl.Element(tm), pl.Element(D_MODEL)),
        lambda b, i: (b, pl.multiple_of(jnp.maximum(i * tm - FRONT, 0), Q_BLOCK), 0))


def _padded_rows(x_ref, meta_ref, h_ref, tm):
    i = pl.program_id(1)

    @pl.when(i == 0)
    def _():
        h_ref[0:FRONT - N_META, :] = jnp.zeros((FRONT - N_META, D_MODEL), F32)
        h_ref[FRONT - N_META:FRONT, :] = meta_ref[...]
        h_ref[FRONT:, :] = x_ref[0:tm - FRONT, :]

    @pl.when(i > 0)
    def _():
        h_ref[...] = x_ref[...]

    return h_ref[...]


def _inproj_kernel(x_ref, meta_ref, g_ref, w_ref, o_ref, h_ref, w16_ref, *, tm):
    chunks = [slice(c0, min(c0 + MXU_WIDTH, PROJ_WIDTH)) for c0 in range(0, PROJ_WIDTH, MXU_WIDTH)]

    @pl.when(jnp.logical_and(pl.program_id(0) == 0, pl.program_id(1) == 0))
    def _():
        for cols in chunks:
            w16_ref[:, cols] = w_ref[cols, :].T.astype(BF16)

    h = _padded_rows(x_ref, meta_ref, h_ref, tm)
    hn = _rms(h, g_ref[...]).astype(BF16)
    for cols in chunks:
        o_ref[:, cols] = _dot(hn, w16_ref[:, cols]).astype(BF16)


def _inproj(x, meta, gain, w_in_t, tm, lp):
    b = x.shape[0]
    kern = functools.partial(_inproj_kernel, tm=tm)
    return pl.pallas_call(
        kern,
        grid=(b, lp // tm),
        in_specs=[
            _x_window(tm),
            _resident((N_META, D_MODEL)),
            _resident((1, D_MODEL)),
            pl.BlockSpec((None, PROJ_WIDTH, D_MODEL), lambda bi, i: (0, 0, 0),
                         pipeline_mode=pl.Buffered(1)),
        ],
        out_specs=pl.BlockSpec((None, tm, PROJ_WIDTH), lambda bi, i: (bi, i, 0)),
        out_shape=jax.ShapeDtypeStruct((b, lp, PROJ_WIDTH), BF16),
        scratch_shapes=[
            pltpu.VMEM((tm, D_MODEL), F32),
            pltpu.VMEM((D_MODEL, PROJ_WIDTH), BF16),
        ],
        compiler_params=pltpu.CompilerParams(
            dimension_semantics=("arbitrary", "arbitrary"), vmem_limit_bytes=VMEM_LIMIT_BYTES),
        name="inproj",
    )(x, meta, gain, w_in_t)


def _sb_tail_matrix(tk):
    j = np.arange(tk)
    later = (j[:, None] > j[None, :]).astype(np.float32)
    half = -np.concatenate([later, np.ones((tk, tk), np.float32)], axis=1)
    return jnp.asarray(np.concatenate([half, half], axis=0), dtype=BF16)


def _sb_kernel(q_ref, k_ref, v_ref, tt_ref, o_ref, acc_ref, c_ref, *, tq, tk):
    i = pl.program_id(2)
    nd = tq // tk
    q = (q_ref[...].astype(F32) * (SB_HEAD_DIM ** -0.5 * LOG2_E)).astype(BF16)
    tt = tt_ref[...]
    head_a = lax.broadcasted_iota(jnp.int32, (tk, 2 * SB_HEAD_DIM), 1) < SB_HEAD_DIM

    def block_diag(x):
        zero = jnp.zeros_like(x)
        return jnp.concatenate([jnp.where(head_a, x, zero), jnp.where(head_a, zero, x)], axis=0)

    def sweep(jb, r0, carry, masked):
        rows = tq - r0
        start = pl.multiple_of(jb * tk, tk)
        k_bd = block_diag(k_ref[pl.ds(start, tk), :])
        v_bd = block_diag(v_ref[pl.ds(start, tk), :])
        z = _dot_nt(q[r0:, :], k_bd)
        s = jnp.maximum(z, 0.0) + jnp.log2(1.0 + jnp.exp2(-jnp.abs(z)))
        log_beta = z - s
        if masked:
            qpos = i * tq + r0 + lax.broadcasted_iota(jnp.int32, (rows, 2 * tk), 0)
            col = lax.broadcasted_iota(jnp.int32, (rows, 2 * tk), 1)
            kpos = jb * tk + jnp.where(col >= tk, col - tk, col)
            strict = jnp.logical_and(kpos < qpos, kpos >= FRONT - N_META)
            s = jnp.where(strict, s, 0.0)
        s_hi, s_lo = _split_hi_lo(s)
        ra = _dot(jnp.concatenate([s_hi[:, :tk], s_lo[:, :tk]], axis=1), tt)
        rb = _dot(jnp.concatenate([s_hi[:, tk:], s_lo[:, tk:]], axis=1), tt)
        tail = jnp.concatenate([ra[:, :tk], rb[:, :tk]], axis=1) + carry
        w = jnp.exp2(log_beta + tail)
        if masked:
            w = jnp.where(strict, w, 0.0)
        carry = carry + jnp.concatenate([ra[:, tk:], rb[:, tk:]], axis=1)
        return carry, _dot(w.astype(BF16), v_bd)

    def row_max(c):
        return jnp.max(jnp.max(c, axis=0, keepdims=True), axis=1, keepdims=True)[0, 0]

    acc_ref[...] = jnp.zeros_like(acc_ref)
    c_ref[...] = jnp.zeros_like(c_ref)
    for d in range(nd - 1, -1, -1):
        r0 = d * tk
        carry, pv = sweep(i * nd + d, r0, c_ref[r0:, :], True)
        c_ref[r0:, :] = carry
        acc_ref[r0:, :] += pv

    def live(jb, cmax, lowest):
        return jnp.logical_and(jb >= lowest, cmax > SB_LOG2_UNDERFLOW)

    def pair(state):
        jb, _ = state
        c1, pv1 = sweep(jb, 0, c_ref[...], False)
        c2, pv2 = sweep(jb - 1, 0, c1, False)
        c_ref[...] = c2
        acc_ref[...] += pv1 + pv2
        return jb - 2, row_max(c2)

    jb, cmax = lax.while_loop(lambda st: live(st[0], st[1], 2), pair,
                              (i * nd - 1, row_max(c_ref[...])))

    def single():
        c1, pv1 = sweep(1, 0, c_ref[...], False)
        c_ref[...] = c1
        acc_ref[...] += pv1
        return row_max(c1)

    odd = live(jb, cmax, 1)
    cmax = lax.cond(odd, single, lambda: cmax)
    jb = jnp.where(odd, 0, jb)

    @pl.when(jnp.logical_and(jb == 0, cmax > SB_LOG2_UNDERFLOW))
    def _():
        _, pv = sweep(0, 0, c_ref[...], True)
        acc_ref[...] += pv

    o_ref[...] = acc_ref[...].astype(o_ref.dtype)


def _stick_breaking(proj3, tq, tk):
    b, lp, _ = proj3.shape
    pair = 2 * SB_HEAD_DIM
    n_pairs = SB_WIDTH // pair
    kern = functools.partial(_sb_kernel, tq=tq, tk=tk)
    return pl.pallas_call(
        kern,
        grid=(b, n_pairs, lp // tq),
        in_specs=[
            pl.BlockSpec((None, tq, pair), lambda bi, hp, i: (bi, i, COL_SBQ // pair + hp)),
            pl.BlockSpec((None, lp, pair), lambda bi, hp, i: (bi, 0, COL_SBK // pair + hp)),
            pl.BlockSpec((None, lp, pair), lambda bi, hp, i: (bi, 0, COL_SBV // pair + hp)),
            _resident((2 * tk, 2 * tk)),
        ],
        out_specs=pl.BlockSpec((None, tq, pair), lambda bi, hp, i: (bi, i, hp)),
        out_shape=jax.ShapeDtypeStruct((b, lp, SB_WIDTH), BF16),
        scratch_shapes=[
            pltpu.VMEM((tq, pair), F32),
            pltpu.VMEM((tq, 2 * tk), F32),
        ],
        compiler_params=pltpu.CompilerParams(
            dimension_semantics=("arbitrary", "arbitrary", "arbitrary"),
            vmem_limit_bytes=VMEM_LIMIT_BYTES),
        name="stick_breaking",
    )(proj3, proj3, proj3, _sb_tail_matrix(tk))


def _gla_constants(c, levels, mxu_levels):
    t = np.arange(c)[:, None]
    j = np.arange(c)[None, :]
    blocks = [(j <= t)]
    for l in range(mxu_levels):
        half = 1 << l
        p = (t >> (l + 1) << (l + 1)) + half
        upper = ((t >> l) & 1) == 1
        blocks.append(np.where(upper, (j >= p) & (j <= t), (j > t) & (j < p)))
    e = np.concatenate(blocks, axis=0).astype(np.float32)
    e2 = np.concatenate([e, e], axis=1)
    s = np.arange(c)[None, :]
    x = t ^ s
    lvl = np.where(x > 0, np.floor(np.log2(np.maximum(x, 1))).astype(np.int32), levels)
    lvl = np.where(s > t, -1, lvl).astype(np.int32)
    return jnp.asarray(e2, dtype=BF16), jnp.asarray(lvl)


def _gla_kernel(q_ref, k_ref, v01_ref, v23_ref, r01_ref, r23_ref, lr_ref, wup_ref, bgk_ref,
                gn_ref, e2_ref, lvl_ref, o_ref, st_ref, *, c, n_chunks, levels, mxu_levels):
    dk, dv = GLA_HEAD_DK, GLA_HEAD_DV
    heads = [slice(h * dk, (h + 1) * dk) for h in range(GLA_HEADS)]
    v_refs = (v01_ref, v01_ref, v23_ref, v23_ref)

    @pl.when(pl.program_id(1) == 0)
    def _():
        st_ref[...] = jnp.zeros_like(st_ref)

    rank_lanes = lax.broadcasted_iota(jnp.int32, (c, LANES), 1) < GATE_RANK

    def chunk(rows):
        lr = jnp.where(rank_lanes, lr_ref[rows, :], jnp.zeros((c, LANES), BF16))
        pre = (_dot(lr, wup_ref[...]) + bgk_ref[...]) * LOG2_E
        g = (jnp.minimum(pre, 0.0) - jnp.log2(1.0 + jnp.exp2(-jnp.abs(pre)))) * (1.0 / GATE_TAU)
        g_hi, g_lo = _split_hi_lo(g)
        x = _dot(e2_ref[...], jnp.concatenate([g_hi, g_lo], axis=0))
        cum = x[:c]

        def level_exponent(l):
            if l < mxu_levels:
                return x[(l + 1) * c:(l + 2) * c]
            half = 1 << l
            parts = []
            for base in range(0, c, 2 * half):
                p = base + half
                ref = jnp.broadcast_to(cum[p - 1:p, :], (half, cum.shape[1]))
                parts.append(ref - cum[base:p])
                parts.append(cum[p:p + half] - ref)
            return jnp.concatenate(parts, axis=0)

        q = q_ref[rows, :].astype(F32) * (dk ** -0.5)
        k = k_ref[rows, :].astype(F32)
        lvl = lvl_ref[...]

        q16, k16 = q.astype(BF16), k.astype(BF16)
        att = [jnp.where(lvl == levels, _dot_nt(q16[:, hs], k16[:, hs]), 0.0) for hs in heads]
        for l in range(levels):
            ex = jnp.exp2(level_exponent(l))
            ql, kl = (q * ex).astype(BF16), (k * ex).astype(BF16)
            att = [jnp.where(lvl == l, _dot_nt(ql[:, hs], kl[:, hs]), a)
                   for hs, a in zip(heads, att)]

        last = cum[c - 1:c, :]
        q_dec = (q * jnp.exp2(cum)).astype(BF16)
        k_dec = (k * jnp.exp2(last - cum)).astype(BF16)
        st_decay = jnp.exp2(last)
        outs = []
        for h, hs in enumerate(heads):
            v = v_refs[h][rows, (h % 2) * dv:(h % 2 + 1) * dv]
            st = st_ref[h]
            o = _dot(att[h].astype(BF16), v) + _dot_nt(q_dec[:, hs], st.astype(BF16))
            st_ref[h] = st * st_decay[:, hs] + lax.dot_general(
                v, k_dec[:, hs], (((0,), (0,)), ((), ())), preferred_element_type=F32)
            outs.append(_rms(o, gn_ref[...]))
        r = jnp.concatenate([r01_ref[rows, :], r23_ref[rows, :]], axis=1).astype(F32)
        o_ref[rows, :] = (jnp.concatenate(outs, axis=1) * (r * jax.nn.sigmoid(r))).astype(o_ref.dtype)

    for n in range(n_chunks):
        chunk(slice(n * c, (n + 1) * c))


def _gla(proj3, w_up_pad, b_gk, head_gain, tm):
    b, lp, _ = proj3.shape
    c, levels, mxu_levels = GLA_CHUNK, GLA_LEVELS, GLA_MXU_LEVELS
    e2, lvl = _gla_constants(c, levels, mxu_levels)
    kw, vw, vb = GLA_K_WIDTH, GLA_V_WIDTH, GLA_VBLOCK
    n_chunks = tm // c
    kern = functools.partial(_gla_kernel, c=c, n_chunks=n_chunks, levels=levels,
                             mxu_levels=mxu_levels)
    return pl.pallas_call(
        kern,
        grid=(b, lp // tm),
        in_specs=[
            pl.BlockSpec((None, tm, kw), lambda bi, t: (bi, t, COL_GQ // kw)),
            pl.BlockSpec((None, tm, kw), lambda bi, t: (bi, t, COL_GK // kw)),
            pl.BlockSpec((None, tm, vb), lambda bi, t: (bi, t, COL_GV // vb)),
            pl.BlockSpec((None, tm, vb), lambda bi, t: (bi, t, COL_GV // vb + 1)),
            pl.BlockSpec((None, tm, vb), lambda bi, t: (bi, t, COL_GR // vb)),
            pl.BlockSpec((None, tm, vb), lambda bi, t: (bi, t, COL_GR // vb + 1)),
            pl.BlockSpec((None, tm, LANES), lambda bi, t: (bi, t, COL_LR // LANES)),
            _resident((LANES, kw)),
            _resident((1, kw)),
            _resident((1, GLA_HEAD_DV)),
            _resident(((mxu_levels + 1) * c, 2 * c)),
            _resident((c, c)),
        ],
        out_specs=pl.BlockSpec((None, tm, vw), lambda bi, t: (bi, t, 0)),
        out_shape=jax.ShapeDtypeStruct((b, lp, vw), BF16),
        scratch_shapes=[pltpu.VMEM((GLA_HEADS, GLA_HEAD_DV, GLA_HEAD_DK), F32)],
        compiler_params=pltpu.CompilerParams(
            dimension_semantics=("arbitrary", "arbitrary"),
            vmem_limit_bytes=VMEM_LIMIT_BYTES),
        name="gla",
    )(proj3, proj3, proj3, proj3, proj3, proj3, proj3, w_up_pad, b_gk, head_gain, e2, lvl)


def _merge_kernel(x_ref, meta_ref, gpre_ref, wm_ref, sb_ref, og_ref, wsb_ref, wgla_ref, wo_ref,
                  gpost_ref, o_ref, h_ref, wm16_ref, *, tm):
    @pl.when(jnp.logical_and(pl.program_id(0) == 0, pl.program_id(1) == 0))
    def _():
        for c0 in range(0, 2 * D_MODEL, MXU_WIDTH):
            cols = slice(c0, c0 + MXU_WIDTH)
            wm16_ref[:, cols] = wm_ref[cols, :].T.astype(BF16)

    h = _padded_rows(x_ref, meta_ref, h_ref, tm)
    hn = _rms(h, gpre_ref[...]).astype(BF16)
    m = _dot(hn, wm16_ref[...])
    a = _dot(sb_ref[...], wsb_ref[...])
    b = _dot(og_ref[...], wgla_ref[...])
    mixed = jax.nn.sigmoid(m[:, :D_MODEL]) * a + jax.nn.sigmoid(m[:, D_MODEL:]) * b
    mix = _dot(mixed.astype(BF16), wo_ref[...])
    o_ref[...] = h + _rms(mix, gpost_ref[...])


def _merge(x, meta, g_pre, w_in_t, sb, og, w_sb, w_gla, w_o, g_post, tm, lp):
    b = x.shape[0]
    kern = functools.partial(_merge_kernel, tm=tm)
    merge_rows = pl.BlockSpec(
        (pl.Squeezed(), pl.Element(2 * D_MODEL), pl.Element(D_MODEL)),
        lambda bi, i: (0, COL_MERGE, 0), pipeline_mode=pl.Buffered(1))
    return pl.pallas_call(
        kern,
        grid=(b, lp // tm),
        in_specs=[
            _x_window(tm),
            _resident((N_META, D_MODEL)),
            _resident((1, D_MODEL)),
            merge_rows,
            pl.BlockSpec((None, tm, SB_WIDTH), lambda bi, i: (bi, i, 0)),
            pl.BlockSpec((None, tm, GLA_V_WIDTH), lambda bi, i: (bi, i, 0)),
            _resident((SB_WIDTH, D_MODEL)),
            _resident((GLA_V_WIDTH, D_MODEL)),
            _resident((D_MODEL, D_MODEL)),
            _resident((1, D_MODEL)),
        ],
        out_specs=pl.BlockSpec((None, tm, D_MODEL), lambda bi, i: (bi, i, 0)),
        out_shape=jax.ShapeDtypeStruct((b, lp, D_MODEL), F32),
        scratch_shapes=[
            pltpu.VMEM((tm, D_MODEL), F32),
            pltpu.VMEM((D_MODEL, 2 * D_MODEL), BF16),
        ],
        compiler_params=pltpu.CompilerParams(
            dimension_semantics=("arbitrary", "arbitrary"), vmem_limit_bytes=VMEM_LIMIT_BYTES),
        name="merge",
    )(x, meta, g_pre, w_in_t, sb, og, w_sb, w_gla, w_o, g_post)


def _ffn_kernel(h_ref, hist_ref, gpre_ref, wup_ref, wgate_ref, cw_ref, cb_ref, wdown_ref,
                gpost_ref, o_ref, tail_ref, act_ref, *, tm):
    chunks = [slice(c0, c0 + FF_CHUNK) for c0 in range(0, D_FF, FF_CHUNK)]

    @pl.when(pl.program_id(1) == 0)
    def _():
        hist = _rms(hist_ref[...], gpre_ref[...]).astype(BF16)
        for cols in chunks:
            tail_ref[:, cols] = _dot(hist, wup_ref[:, cols])

    h = h_ref[...]
    hn = _rms(h, gpre_ref[...]).astype(BF16)
    row = lax.broadcasted_iota(jnp.int32, (SUBLANES, FF_CHUNK), 0)
    for cols in chunks:
        up = _dot(hn, wup_ref[:, cols])
        gate = _dot(hn, wgate_ref[:, cols])
        prev = tail_ref[:, cols]
        r1 = pltpu.roll(up, 1, 0)
        r2 = pltpu.roll(up, 2, 0)
        head1 = jnp.where(row == 0, prev[7:8, :], r1[:SUBLANES])
        head2 = jnp.where(row == 0, prev[6:7, :], jnp.where(row == 1, prev[7:8, :], r2[:SUBLANES]))
        up1 = jnp.concatenate([head1, r1[SUBLANES:]], axis=0)
        up2 = jnp.concatenate([head2, r2[SUBLANES:]], axis=0)
        tail_ref[:, cols] = up[tm - SUBLANES:, :]
        cw = cw_ref[:, cols]
        y = cw[0:1, :] * up2 + cw[1:2, :] * up1 + cw[2:3, :] * up + cb_ref[:, cols]
        act_ref[:, cols] = (jax.nn.gelu(y, approximate=True) * gate).astype(BF16)
    ffn = _dot(act_ref[...], wdown_ref[...])
    o_ref[...] = h + _rms(ffn, gpost_ref[...])


def _ffn(h3, g_pre, w_up, w_gate, conv_w, conv_b, w_down, g_post, tm):
    b, lp, _ = h3.shape
    s = lp - FRONT
    kern = functools.partial(_ffn_kernel, tm=tm)

    def rows(n, start):
        return pl.BlockSpec((pl.Squeezed(), pl.Element(n), pl.Element(D_MODEL)), start)

    return pl.pallas_call(
        kern,
        grid=(b, s // tm),
        in_specs=[
            rows(tm, lambda bi, i: (bi, pl.multiple_of(FRONT + i * tm, Q_BLOCK), 0)),
            rows(SUBLANES, lambda bi, i: (bi, FRONT - SUBLANES, 0)),
            _resident((1, D_MODEL)),
            _resident((D_MODEL, D_FF)),
            _resident((D_MODEL, D_FF)),
            _resident((CONV_W, D_FF)),
            _resident((1, D_FF)),
            _resident((D_FF, D_MODEL)),
            _resident((1, D_MODEL)),
        ],
        out_specs=pl.BlockSpec((None, tm, D_MODEL), lambda bi, i: (bi, i, 0)),
        out_shape=jax.ShapeDtypeStruct((b, s, D_MODEL), F32),
        scratch_shapes=[
            pltpu.VMEM((SUBLANES, D_FF), F32),
            pltpu.VMEM((tm, D_FF), BF16),
        ],
        compiler_params=pltpu.CompilerParams(
            dimension_semantics=("arbitrary", "arbitrary"), vmem_limit_bytes=VMEM_LIMIT_BYTES),
        name="ffn",
    )(h3, h3, g_pre, w_up, w_gate, conv_w, conv_b, w_down, g_post)


def _pick_tile(n, want):
    t = want
    while n % t:
        t -= Q_BLOCK
    return t


def kernel(x, meta_tokens, norm_mix_pre, w_in, w_gk_up, b_gk, gla_head_norm, w_sb_out, w_gla_out, w_o,
           norm_mix_post, norm_ffn_pre, w_ffn_up, w_ffn_gate, conv_w, conv_b, w_ffn_down, norm_ffn_post):
    b, s, _ = x.shape
    assert s % Q_BLOCK == 0 and norm_mix_pre.shape[0] == 1
    lp = FRONT + s
    tm = _pick_tile(lp, ROW_TILE)
    tq = _pick_tile(lp, SB_TQ)
    tf = _pick_tile(s, FFN_TILE)
    d = 0

    w_in_t = jnp.swapaxes(w_in, 1, 2)
    w_up_pad = jnp.pad(w_gk_up[d].astype(BF16), ((0, LANES - GATE_RANK), (0, 0)))

    proj3 = _inproj(x, meta_tokens, norm_mix_pre[d][None, :], w_in_t, tm, lp)
    sb = _stick_breaking(proj3, tq, SB_TK)
    og = _gla(proj3, w_up_pad, b_gk[d][None, :], gla_head_norm[d][None, :], tm)
    h1 = _merge(x, meta_tokens, norm_mix_pre[d][None, :], w_in_t, sb, og,
                w_sb_out[d].astype(BF16), w_gla_out[d].astype(BF16), w_o[d].astype(BF16),
                norm_mix_post[d][None, :], tm, lp)
    return _ffn(h1, norm_ffn_pre[d][None, :], w_ffn_up[d].astype(BF16), w_ffn_gate[d].astype(BF16),
                conv_w[d], conv_b[d][None, :], w_ffn_down[d].astype(BF16),
                norm_ffn_post[d][None, :], tf)
```

```python
import functools

import numpy as np
import jax
import jax.numpy as jnp
from jax import lax
from jax.experimental import pallas as pl
from jax.experimental.pallas import tpu as pltpu

F32 = jnp.float32
BF16 = jnp.bfloat16

D_MODEL = 1024
N_META = 16
Q_BLOCK = 128
SB_HEADS = 8
SB_HEAD_DIM = 64
GLA_HEADS = 4
GLA_HEAD_DK = 128
GLA_HEAD_DV = 256
GATE_RANK = 16
GATE_TAU = 16.0
D_FF = 2816
CONV_W = 3
EPS = 1e-6

SB_WIDTH = SB_HEADS * SB_HEAD_DIM
GLA_K_WIDTH = GLA_HEADS * GLA_HEAD_DK
GLA_V_WIDTH = GLA_HEADS * GLA_HEAD_DV

LANES = 128
SUBLANES = 8
MXU_WIDTH = 256
VMEM_LIMIT_BYTES = 56 * 1024 * 1024

COL_SBQ = 0
COL_SBK = COL_SBQ + SB_WIDTH
COL_SBV = COL_SBK + SB_WIDTH
COL_GQ = COL_SBV + SB_WIDTH
COL_GK = COL_GQ + GLA_K_WIDTH
COL_GV = COL_GK + GLA_K_WIDTH
COL_GR = COL_GV + GLA_V_WIDTH
COL_LR = COL_GR + GLA_V_WIDTH
COL_MERGE = COL_LR + GATE_RANK
PROJ_WIDTH = COL_LR + LANES

FRONT = Q_BLOCK
ROW_TILE = 640
FFN_TILE = 1024
SB_TQ = 640
SB_TK = 128
GLA_CHUNK = 128
GLA_LEVELS = 7
GLA_MXU_LEVELS = 3
GLA_VBLOCK = 512
FF_CHUNK = 256
LOG2_E = 1.4426950408889634
SB_LOG2_UNDERFLOW = -126.0
SB_MASKED = -1e30


def _dot(a, b):
    return jnp.dot(a, b, preferred_element_type=F32)


def _dot_nt(a, b):
    return lax.dot_general(a, b, (((1,), (1,)), ((), ())), preferred_element_type=F32)


def _rms(x, gain):
    ms = jnp.mean(x * x, axis=-1, keepdims=True)
    return x * lax.rsqrt(ms + EPS) * gain


def _split_hi_lo(x):
    hi = x.astype(BF16)
    lo = (x - hi.astype(F32)).astype(BF16)
    return hi, lo


def _resident(shape):
    nd = len(shape)
    return pl.BlockSpec(shape, lambda *_: (0,) * nd, pipeline_mode=pl.Buffered(1))


def _x_window(tm):
    return pl.BlockSpec(
        (pl.Squeezed(), pl.Element(tm), pl.Element(D_MODEL)),
        lambda b, i: (b, pl.multiple_of(jnp.maximum(i * tm - FRONT, 0), Q_BLOCK), 0))


def _padded_rows(x_ref, meta_ref, h_ref, tm):
    i = pl.program_id(1)

    @pl.when(i == 0)
    def _():
        h_ref[0:FRONT - N_META, :] = jnp.zeros((FRONT - N_META, D_MODEL), F32)
        h_ref[FRONT - N_META:FRONT, :] = meta_ref[...]
        h_ref[FRONT:, :] = x_ref[0:tm - FRONT, :]

    @pl.when(i > 0)
    def _():
        h_ref[...] = x_ref[...]

    return h_ref[...]


def _inproj_kernel(x_ref, meta_ref, g_ref, w_ref, o_ref, h_ref, w16_ref, *, tm):
    chunks = [slice(c0, min(c0 + MXU_WIDTH, PROJ_WIDTH)) for c0 in range(0, PROJ_WIDTH, MXU_WIDTH)]

    @pl.when(jnp.logical_and(pl.program_id(0) == 0, pl.program_id(1) == 0))
    def _():
        for cols in chunks:
            w16_ref[:, cols] = w_ref[cols, :].T.astype(BF16)

    h = _padded_rows(x_ref, meta_ref, h_ref, tm)
    hn = _rms(h, g_ref[...]).astype(BF16)
    for cols in chunks:
        o_ref[:, cols] = _dot(hn, w16_ref[:, cols]).astype(BF16)


def _inproj(x, meta, gain, w_in_t, tm, lp):
    b = x.shape[0]
    kern = functools.partial(_inproj_kernel, tm=tm)
    return pl.pallas_call(
        kern,
        grid=(b, lp // tm),
        in_specs=[
            _x_window(tm),
            _resident((N_META, D_MODEL)),
            _resident((1, D_MODEL)),
            pl.BlockSpec((None, PROJ_WIDTH, D_MODEL), lambda bi, i: (0, 0, 0),
                         pipeline_mode=pl.Buffered(1)),
        ],
        out_specs=pl.BlockSpec((None, tm, PROJ_WIDTH), lambda bi, i: (bi, i, 0)),
        out_shape=jax.ShapeDtypeStruct((b, lp, PROJ_WIDTH), BF16),
        scratch_shapes=[
            pltpu.VMEM((tm, D_MODEL), F32),
            pltpu.VMEM((D_MODEL, PROJ_WIDTH), BF16),
        ],
        compiler_params=pltpu.CompilerParams(
            dimension_semantics=("arbitrary", "arbitrary"), vmem_limit_bytes=VMEM_LIMIT_BYTES),
        name="inproj",
    )(x, meta, gain, w_in_t)


def _sb_tail_matrix(tk):
    j = np.arange(tk)
    later = (j[:, None] > j[None, :]).astype(np.float32)
    half = -np.concatenate([later, np.ones((tk, tk), np.float32)], axis=1)
    return jnp.asarray(np.concatenate([half, half], axis=0), dtype=BF16)


def _sb_kernel(q_ref, k_ref, v_ref, tt_ref, o_ref, acc_ref, c_ref, *, tq, tk):
    i = pl.program_id(2)
    nd = tq // tk
    q = (q_ref[...].astype(F32) * (SB_HEAD_DIM ** -0.5 * LOG2_E)).astype(BF16)
    tt = tt_ref[...]
    head_a = lax.broadcasted_iota(jnp.int32, (tk, 2 * SB_HEAD_DIM), 1) < SB_HEAD_DIM
    col = lax.broadcasted_iota(jnp.int32, (tk, 2 * tk), 1)
    key = jnp.where(col >= tk, col - tk, col)
    row = lax.broadcasted_iota(jnp.int32, (tk, 2 * tk), 0)
    causal_bias = jnp.where(key < row, 0.0, SB_MASKED)

    def block_diag(x):
        zero = jnp.zeros_like(x)
        return jnp.concatenate([jnp.where(head_a, x, zero), jnp.where(head_a, zero, x)], axis=0)

    def step(delta, diagonal):
        zs, v_bds = [], []
        for r in range(nd):
            jb = i * nd + r - delta
            start = pl.multiple_of(jnp.maximum(jb, 0) * tk, tk)
            k_bd = block_diag(k_ref[pl.ds(start, tk), :])
            v_bds.append(block_diag(v_ref[pl.ds(start, tk), :]))
            z = _dot_nt(q[r * tk:(r + 1) * tk, :], k_bd)
            first_key = jnp.where(jb == 0, FRONT - N_META, jnp.where(jb < 0, tk, 0))
            z = z + jnp.where(key[0:1, :] >= first_key, 0.0, SB_MASKED)
            if diagonal:
                z = z + causal_bias
            zs.append(z)
        z = jnp.concatenate(zs, axis=0)
        s = jnp.maximum(z, 0.0) + jnp.log2(1.0 + jnp.exp2(-jnp.abs(z)))
        log_beta = z - s
        s_hi, s_lo = _split_hi_lo(s)
        ra = _dot(jnp.concatenate([s_hi[:, :tk], s_lo[:, :tk]], axis=1), tt)
        rb = _dot(jnp.concatenate([s_hi[:, tk:], s_lo[:, tk:]], axis=1), tt)
        carry = c_ref[...]
        tail = jnp.concatenate([ra[:, :tk], rb[:, :tk]], axis=1) + carry
        w = jnp.exp2(log_beta + tail).astype(BF16)
        carry = carry + jnp.concatenate([ra[:, tk:], rb[:, tk:]], axis=1)
        c_ref[...] = carry
        acc_ref[...] += jnp.concatenate(
            [_dot(w[r * tk:(r + 1) * tk, :], v_bds[r]) for r in range(nd)], axis=0)
        return jnp.max(jnp.max(carry, axis=0, keepdims=True), axis=1, keepdims=True)[0, 0]

    acc_ref[...] = jnp.zeros_like(acc_ref)
    c_ref[...] = jnp.zeros_like(c_ref)
    stick = step(0, True)

    def live(state):
        delta, stick = state
        return jnp.logical_and(delta < (i + 1) * nd, stick > SB_LOG2_UNDERFLOW)

    lax.while_loop(live, lambda state: (state[0] + 1, step(state[0], False)), (1, stick))
    o_ref[...] = acc_ref[...].astype(o_ref.dtype)


def _stick_breaking(proj3, tq, tk):
    b, lp, _ = proj3.shape
    pair = 2 * SB_HEAD_DIM
    n_pairs = SB_WIDTH // pair
    kern = functools.partial(_sb_kernel, tq=tq, tk=tk)
    return pl.pallas_call(
        kern,
        grid=(b, n_pairs, lp // tq),
        in_specs=[
            pl.BlockSpec((None, tq, pair), lambda bi, hp, i: (bi, i, COL_SBQ // pair + hp)),
            pl.BlockSpec((None, lp, pair), lambda bi, hp, i: (bi, 0, COL_SBK // pair + hp)),
            pl.BlockSpec((None, lp, pair), lambda bi, hp, i: (bi, 0, COL_SBV // pair + hp)),
            _resident((2 * tk, 2 * tk)),
        ],
        out_specs=pl.BlockSpec((None, tq, pair), lambda bi, hp, i: (bi, i, hp)),
        out_shape=jax.ShapeDtypeStruct((b, lp, SB_WIDTH), BF16),
        scratch_shapes=[
            pltpu.VMEM((tq, pair), F32),
            pltpu.VMEM((tq, 2 * tk), F32),
        ],
        compiler_params=pltpu.CompilerParams(
            dimension_semantics=("arbitrary", "arbitrary", "arbitrary"),
            vmem_limit_bytes=VMEM_LIMIT_BYTES),
        name="stick_breaking",
    )(proj3, proj3, proj3, _sb_tail_matrix(tk))


def _gla_constants(c, levels, mxu_levels):
    t = np.arange(c)[:, None]
    j = np.arange(c)[None, :]
    blocks = [(j <= t)]
    for l in range(mxu_levels):
        half = 1 << l
        p = (t >> (l + 1) << (l + 1)) + half
        upper = ((t >> l) & 1) == 1
        blocks.append(np.where(upper, (j >= p) & (j <= t), (j > t) & (j < p)))
    e = np.concatenate(blocks, axis=0).astype(np.float32)
    e2 = np.concatenate([e, e], axis=1)
    s = np.arange(c)[None, :]
    x = t ^ s
    lvl = np.where(x > 0, np.floor(np.log2(np.maximum(x, 1))).astype(np.int32), levels)
    lvl = np.where(s > t, -1, lvl).astype(np.int32)
    return jnp.asarray(e2, dtype=BF16), jnp.asarray(lvl)


def _gla_kernel(q_ref, k_ref, v01_ref, v23_ref, r01_ref, r23_ref, lr_ref, wup_ref, bgk_ref,
                gn_ref, e2_ref, lvl_ref, o_ref, st_ref, *, c, n_chunks, levels, mxu_levels):
    dk, dv = GLA_HEAD_DK, GLA_HEAD_DV
    heads = [slice(h * dk, (h + 1) * dk) for h in range(GLA_HEADS)]
    v_refs = (v01_ref, v01_ref, v23_ref, v23_ref)

    @pl.when(pl.program_id(1) == 0)
    def _():
        st_ref[...] = jnp.zeros_like(st_ref)

    rank_lanes = lax.broadcasted_iota(jnp.int32, (c, LANES), 1) < GATE_RANK
    pos = lax.broadcasted_iota(jnp.int32, (c, GLA_K_WIDTH), 0)
    row_bit = [((pos >> l) & 1) == 1 for l in range(mxu_levels)]

    def chunk(rows):
        lr = jnp.where(rank_lanes, lr_ref[rows, :], jnp.zeros((c, LANES), BF16))
        pre = (_dot(lr, wup_ref[...]) + bgk_ref[...]) * LOG2_E
        g = (jnp.minimum(pre, 0.0) - jnp.log2(1.0 + jnp.exp2(-jnp.abs(pre)))) * (1.0 / GATE_TAU)
        g_hi, g_lo = _split_hi_lo(g)
        x = _dot(e2_ref[...], jnp.concatenate([g_hi, g_lo], axis=0))
        cum = x[:c]

        def level_exponent(l):
            if l < mxu_levels:
                return x[(l + 1) * c:(l + 2) * c]
            half = 1 << l
            parts = []
            for base in range(0, c, 2 * half):
                p = base + half
                ref = jnp.broadcast_to(cum[p - 1:p, :], (half, cum.shape[1]))
                parts.append(ref - cum[base:p])
                parts.append(cum[p:p + half] - ref)
            return jnp.concatenate(parts, axis=0)

        q = q_ref[rows, :].astype(F32) * (dk ** -0.5)
        k = k_ref[rows, :].astype(F32)
        lvl = lvl_ref[...]

        def query_or_key(l):
            half = 1 << l
            if half >= SUBLANES:
                return jnp.concatenate(
                    [(q if (r0 // half) % 2 else k)[r0:r0 + half] for r0 in range(0, c, half)],
                    axis=0)
            return jnp.where(row_bit[l], q, k)

        q16, k16 = q.astype(BF16), k.astype(BF16)
        att = [jnp.where(lvl == levels, _dot_nt(q16[:, hs], k16[:, hs]), 0.0) for hs in heads]
        for l in range(levels):
            qk = (query_or_key(l) * jnp.exp2(level_exponent(l))).astype(BF16)
            att = [jnp.where(lvl == l, _dot_nt(qk[:, hs], qk[:, hs]), a)
                   for hs, a in zip(heads, att)]

        last = cum[c - 1:c, :]
        q_dec = (q * jnp.exp2(cum)).astype(BF16)
        k_dec = (k * jnp.exp2(last - cum)).astype(BF16)
        st_decay = jnp.exp2(last)
        outs = []
        for h, hs in enumerate(heads):
            v = v_refs[h][rows, (h % 2) * dv:(h % 2 + 1) * dv]
            st = st_ref[h]
            o = _dot(att[h].astype(BF16), v) + _dot_nt(q_dec[:, hs], st.astype(BF16))
            st_ref[h] = st * st_decay[:, hs] + lax.dot_general(
                v, k_dec[:, hs], (((0,), (0,)), ((), ())), preferred_element_type=F32)
            outs.append(_rms(o, gn_ref[...]))
        r = jnp.concatenate([r01_ref[rows, :], r23_ref[rows, :]], axis=1).astype(F32)
        o_ref[rows, :] = (jnp.concatenate(outs, axis=1) * (r * jax.nn.sigmoid(r))).astype(o_ref.dtype)

    for n in range(n_chunks):
        chunk(slice(n * c, (n + 1) * c))


def _gla(proj3, w_up_pad, b_gk, head_gain, tm):
    b, lp, _ = proj3.shape
    c, levels, mxu_levels = GLA_CHUNK, GLA_LEVELS, GLA_MXU_LEVELS
    e2, lvl = _gla_constants(c, levels, mxu_levels)
    kw, vw, vb = GLA_K_WIDTH, GLA_V_WIDTH, GLA_VBLOCK
    n_chunks = tm // c
    kern = functools.partial(_gla_kernel, c=c, n_chunks=n_chunks, levels=levels,
                             mxu_levels=mxu_levels)
    return pl.pallas_call(
        kern,
        grid=(b, lp // tm),
        in_specs=[
            pl.BlockSpec((None, tm, kw), lambda bi, t: (bi, t, COL_GQ // kw)),
            pl.BlockSpec((None, tm, kw), lambda bi, t: (bi, t, COL_GK // kw)),
            pl.BlockSpec((None, tm, vb), lambda bi, t: (bi, t, COL_GV // vb)),
            pl.BlockSpec((None, tm, vb), lambda bi, t: (bi, t, COL_GV // vb + 1)),
            pl.BlockSpec((None, tm, vb), lambda bi, t: (bi, t, COL_GR // vb)),
            pl.BlockSpec((None, tm, vb), lambda bi, t: (bi, t, COL_GR // vb + 1)),
            pl.BlockSpec((None, tm, LANES), lambda bi, t: (bi, t, COL_LR // LANES)),
            _resident((LANES, kw)),
            _resident((1, kw)),
            _resident((1, GLA_HEAD_DV)),
            _resident(((mxu_levels + 1) * c, 2 * c)),
            _resident((c, c)),
        ],
        out_specs=pl.BlockSpec((None, tm, vw), lambda bi, t: (bi, t, 0)),
        out_shape=jax.ShapeDtypeStruct((b, lp, vw), BF16),
        scratch_shapes=[pltpu.VMEM((GLA_HEADS, GLA_HEAD_DV, GLA_HEAD_DK), F32)],
        compiler_params=pltpu.CompilerParams(
            dimension_semantics=("arbitrary", "arbitrary"),
            vmem_limit_bytes=VMEM_LIMIT_BYTES),
        name="gla",
    )(proj3, proj3, proj3, proj3, proj3, proj3, proj3, w_up_pad, b_gk, head_gain, e2, lvl)


def _merge_kernel(x_ref, meta_ref, gpre_ref, wm_ref, sb_ref, og_ref, wsb_ref, wgla_ref, wo_ref,
                  gpost_ref, o_ref, h_ref, wm16_ref, *, tm):
    @pl.when(jnp.logical_and(pl.program_id(0) == 0, pl.program_id(1) == 0))
    def _():
        for c0 in range(0, 2 * D_MODEL, MXU_WIDTH):
            cols = slice(c0, c0 + MXU_WIDTH)
            wm16_ref[:, cols] = wm_ref[cols, :].T.astype(BF16)

    h = _padded_rows(x_ref, meta_ref, h_ref, tm)
    hn = _rms(h, gpre_ref[...]).astype(BF16)
    m = _dot(hn, wm16_ref[...])
    a = _dot(sb_ref[...], wsb_ref[...])
    b = _dot(og_ref[...], wgla_ref[...])
    mixed = jax.nn.sigmoid(m[:, :D_MODEL]) * a + jax.nn.sigmoid(m[:, D_MODEL:]) * b
    mix = _dot(mixed.astype(BF16), wo_ref[...])
    o_ref[...] = h + _rms(mix, gpost_ref[...])


def _merge(x, meta, g_pre, w_in_t, sb, og, w_sb, w_gla, w_o, g_post, tm, lp):
    b = x.shape[0]
    kern = functools.partial(_merge_kernel, tm=tm)
    merge_rows = pl.BlockSpec(
        (pl.Squeezed(), pl.Element(2 * D_MODEL), pl.Element(D_MODEL)),
        lambda bi, i: (0, COL_MERGE, 0), pipeline_mode=pl.Buffered(1))
    return pl.pallas_call(
        kern,
        grid=(b, lp // tm),
        in_specs=[
            _x_window(tm),
            _resident((N_META, D_MODEL)),
            _resident((1, D_MODEL)),
            merge_rows,
            pl.BlockSpec((None, tm, SB_WIDTH), lambda bi, i: (bi, i, 0)),
            pl.BlockSpec((None, tm, GLA_V_WIDTH), lambda bi, i: (bi, i, 0)),
            _resident((SB_WIDTH, D_MODEL)),
            _resident((GLA_V_WIDTH, D_MODEL)),
            _resident((D_MODEL, D_MODEL)),
            _resident((1, D_MODEL)),
        ],
        out_specs=pl.BlockSpec((None, tm, D_MODEL), lambda bi, i: (bi, i, 0)),
        out_shape=jax.ShapeDtypeStruct((b, lp, D_MODEL), F32),
        scratch_shapes=[
            pltpu.VMEM((tm, D_MODEL), F32),
            pltpu.VMEM((D_MODEL, 2 * D_MODEL), BF16),
        ],
        compiler_params=pltpu.CompilerParams(
            dimension_semantics=("arbitrary", "arbitrary"), vmem_limit_bytes=VMEM_LIMIT_BYTES),
        name="merge",
    )(x, meta, g_pre, w_in_t, sb, og, w_sb, w_gla, w_o, g_post)


def _ffn_kernel(h_ref, hist_ref, gpre_ref, wup_ref, wgate_ref, cw_ref, cb_ref, wdown_ref,
                gpost_ref, o_ref, tail_ref, act_ref, *, tm):
    chunks = [slice(c0, c0 + FF_CHUNK) for c0 in range(0, D_FF, FF_CHUNK)]

    @pl.when(pl.program_id(1) == 0)
    def _():
        hist = _rms(hist_ref[...], gpre_ref[...]).astype(BF16)
        for cols in chunks:
            tail_ref[:, cols] = _dot(hist, wup_ref[:, cols])

    h = h_ref[...]
    hn = _rms(h, gpre_ref[...]).astype(BF16)
    row = lax.broadcasted_iota(jnp.int32, (SUBLANES, FF_CHUNK), 0)
    for cols in chunks:
        up = _dot(hn, wup_ref[:, cols])
        gate = _dot(hn, wgate_ref[:, cols])
        prev = tail_ref[:, cols]
        r1 = pltpu.roll(up, 1, 0)
        r2 = pltpu.roll(up, 2, 0)
        head1 = jnp.where(row == 0, prev[7:8, :], r1[:SUBLANES])
        head2 = jnp.where(row == 0, prev[6:7, :], jnp.where(row == 1, prev[7:8, :], r2[:SUBLANES]))
        up1 = jnp.concatenate([head1, r1[SUBLANES:]], axis=0)
        up2 = jnp.concatenate([head2, r2[SUBLANES:]], axis=0)
        tail_ref[:, cols] = up[tm - SUBLANES:, :]
        cw = cw_ref[:, cols]
        y = cw[0:1, :] * up2 + cw[1:2, :] * up1 + cw[2:3, :] * up + cb_ref[:, cols]
        act_ref[:, cols] = (jax.nn.gelu(y, approximate=True) * gate).astype(BF16)
    ffn = _dot(act_ref[...], wdown_ref[...])
    o_ref[...] = h + _rms(ffn, gpost_ref[...])


def _ffn(h3, g_pre, w_up, w_gate, conv_w, conv_b, w_down, g_post, tm):
    b, lp, _ = h3.shape
    s = lp - FRONT
    kern = functools.partial(_ffn_kernel, tm=tm)

    def rows(n, start):
        return pl.BlockSpec((pl.Squeezed(), pl.Element(n), pl.Element(D_MODEL)), start)

    return pl.pallas_call(
        kern,
        grid=(b, s // tm),
        in_specs=[
            rows(tm, lambda bi, i: (bi, pl.multiple_of(FRONT + i * tm, Q_BLOCK), 0)),
            rows(SUBLANES, lambda bi, i: (bi, FRONT - SUBLANES, 0)),
            _resident((1, D_MODEL)),
            _resident((D_MODEL, D_FF)),
            _resident((D_MODEL, D_FF)),
            _resident((CONV_W, D_FF)),
            _resident((1, D_FF)),
            _resident((D_FF, D_MODEL)),
            _resident((1, D_MODEL)),
        ],
        out_specs=pl.BlockSpec((None, tm, D_MODEL), lambda bi, i: (bi, i, 0)),
        out_shape=jax.ShapeDtypeStruct((b, s, D_MODEL), F32),
        scratch_shapes=[
            pltpu.VMEM((SUBLANES, D_FF), F32),
            pltpu.VMEM((tm, D_FF), BF16),
        ],
        compiler_params=pltpu.CompilerParams(
            dimension_semantics=("arbitrary", "arbitrary"), vmem_limit_bytes=VMEM_LIMIT_BYTES),
        name="ffn",
    )(h3, h3, g_pre, w_up, w_gate, conv_w, conv_b, w_down, g_post)


def _pick_tile(n, want):
    t = want
    while n % t:
        t -= Q_BLOCK
    return t


def kernel(x, meta_tokens, norm_mix_pre, w_in, w_gk_up, b_gk, gla_head_norm, w_sb_out, w_gla_out, w_o,
           norm_mix_post, norm_ffn_pre, w_ffn_up, w_ffn_gate, conv_w, conv_b, w_ffn_down, norm_ffn_post):
    b, s, _ = x.shape
    assert s % Q_BLOCK == 0 and norm_mix_pre.shape[0] == 1
    lp = FRONT + s
    tm = _pick_tile(lp, ROW_TILE)
    tq = _pick_tile(lp, SB_TQ)
    tf = _pick_tile(s, FFN_TILE)
    d = 0

    w_in_t = jnp.swapaxes(w_in, 1, 2)
    w_up_pad = jnp.pad(w_gk_up[d].astype(BF16), ((0, LANES - GATE_RANK), (0, 0)))

    proj3 = _inproj(x, meta_tokens, norm_mix_pre[d][None, :], w_in_t, tm, lp)
    sb = _stick_breaking(proj3, tq, SB_TK)
    og = _gla(proj3, w_up_pad, b_gk[d][None, :], gla_head_norm[d][None, :], tm)
    h1 = _merge(x, meta_tokens, norm_mix_pre[d][None, :], w_in_t, sb, og,
                w_sb_out[d].astype(BF16), w_gla_out[d].astype(BF16), w_o[d].astype(BF16),
                norm_mix_post[d][None, :], tm, lp)
    return _ffn(h1, norm_ffn_pre[d][None, :], w_ffn_up[d].astype(BF16), w_ffn_gate[d].astype(BF16),
                conv_w[d], conv_b[d][None, :], w_ffn_down[d].astype(BF16),
                norm_ffn_post[d][None, :], tf)
```

```python
import functools

import numpy as np
import jax
import jax.numpy as jnp
from jax import lax
from jax.experimental import pallas as pl
from jax.experimental.pallas import tpu as pltpu

F32 = jnp.float32
BF16 = jnp.bfloat16

D_MODEL = 1024
N_META = 16
Q_BLOCK = 128
SB_HEADS = 8
SB_HEAD_DIM = 64
GLA_HEADS = 4
GLA_HEAD_DK = 128
GLA_HEAD_DV = 256
GATE_RANK = 16
GATE_TAU = 16.0
D_FF = 2816
CONV_W = 3
EPS = 1e-6

SB_WIDTH = SB_HEADS * SB_HEAD_DIM
GLA_K_WIDTH = GLA_HEADS * GLA_HEAD_DK
GLA_V_WIDTH = GLA_HEADS * GLA_HEAD_DV

LANES = 128
SUBLANES = 8
MXU_WIDTH = 256
VMEM_LIMIT_BYTES = 56 * 1024 * 1024

COL_SBQ = 0
COL_SBK = COL_SBQ + SB_WIDTH
COL_SBV = COL_SBK + SB_WIDTH
COL_GQ = COL_SBV + SB_WIDTH
COL_GK = COL_GQ + GLA_K_WIDTH
COL_GV = COL_GK + GLA_K_WIDTH
COL_GR = COL_GV + GLA_V_WIDTH
COL_LR = COL_GR + GLA_V_WIDTH
COL_MERGE = COL_LR + GATE_RANK
PROJ_WIDTH = COL_LR + LANES

FRONT = Q_BLOCK
ROW_TILE = 640
FFN_TILE = 1024
SB_TQ = 640
SB_TK = 128
GLA_CHUNK = 128
GLA_LEVELS = 7
GLA_MXU_LEVELS = 3
GLA_VBLOCK = 512
FF_CHUNK = 256
LOG2_E = 1.4426950408889634
SB_LOG2_UNDERFLOW = -126.0
SB_MASKED = -1e30


def _dot(a, b):
    return jnp.dot(a, b, preferred_element_type=F32)


def _dot_nt(a, b):
    return lax.dot_general(a, b, (((1,), (1,)), ((), ())), preferred_element_type=F32)


def _rms(x, gain):
    ms = jnp.mean(x * x, axis=-1, keepdims=True)
    return x * lax.rsqrt(ms + EPS) * gain


def _split_hi_lo(x):
    hi = x.astype(BF16)
    lo = (x - hi.astype(F32)).astype(BF16)
    return hi, lo


def _resident(shape):
    nd = len(shape)
    return pl.BlockSpec(shape, lambda *_: (0,) * nd, pipeline_mode=pl.Buffered(1))


def _x_window(tm):
    return pl.BlockSpec(
        (pl.Squeezed(), pl.Element(tm), pl.Element(D_MODEL)),
        lambda b, i: (b, pl.multiple_of(jnp.maximum(i * tm - FRONT, 0), Q_BLOCK), 0))


def _padded_rows(x_ref, meta_ref, h_ref, tm):
    i = pl.program_id(1)

    @pl.when(i == 0)
    def _():
        h_ref[0:FRONT - N_META, :] = jnp.zeros((FRONT - N_META, D_MODEL), F32)
        h_ref[FRONT - N_META:FRONT, :] = meta_ref[...]
        h_ref[FRONT:, :] = x_ref[0:tm - FRONT, :]

    @pl.when(i > 0)
    def _():
        h_ref[...] = x_ref[...]

    return h_ref[...]


def _inproj_kernel(x_ref, meta_ref, g_ref, w_ref, o_ref, h_ref, w16_ref, *, tm):
    chunks = [slice(c0, min(c0 + MXU_WIDTH, PROJ_WIDTH)) for c0 in range(0, PROJ_WIDTH, MXU_WIDTH)]

    @pl.when(jnp.logical_and(pl.program_id(0) == 0, pl.program_id(1) == 0))
    def _():
        for cols in chunks:
            w16_ref[:, cols] = w_ref[cols, :].T.astype(BF16)

    h = _padded_rows(x_ref, meta_ref, h_ref, tm)
    hn = _rms(h, g_ref[...]).astype(BF16)
    for cols in chunks:
        o_ref[:, cols] = _dot(hn, w16_ref[:, cols]).astype(BF16)


def _inproj(x, meta, gain, w_in_t, tm, lp):
    b = x.shape[0]
    kern = functools.partial(_inproj_kernel, tm=tm)
    return pl.pallas_call(
        kern,
        grid=(b, lp // tm),
        in_specs=[
            _x_window(tm),
            _resident((N_META, D_MODEL)),
            _resident((1, D_MODEL)),
            pl.BlockSpec((None, PROJ_WIDTH, D_MODEL), lambda bi, i: (0, 0, 0),
                         pipeline_mode=pl.Buffered(1)),
        ],
        out_specs=pl.BlockSpec((None, tm, PROJ_WIDTH), lambda bi, i: (bi, i, 0)),
        out_shape=jax.ShapeDtypeStruct((b, lp, PROJ_WIDTH), BF16),
        scratch_shapes=[
            pltpu.VMEM((tm, D_MODEL), F32),
            pltpu.VMEM((D_MODEL, PROJ_WIDTH), BF16),
        ],
        compiler_params=pltpu.CompilerParams(
            dimension_semantics=("arbitrary", "arbitrary"), vmem_limit_bytes=VMEM_LIMIT_BYTES),
        name="inproj",
    )(x, meta, gain, w_in_t)


def _sb_tail_matrix(tk):
    j = np.arange(tk)
    later = (j[:, None] > j[None, :]).astype(np.float32)
    ones, zero = np.ones((tk, tk), np.float32), np.zeros((tk, tk), np.float32)
    return jnp.asarray(-np.block([[later, zero, ones, zero], [zero, later, zero, ones]]), dtype=BF16)


def _sb_kernel(q_ref, k_ref, v_ref, tt_ref, o_ref, acc_ref, c_ref, *, tq, tk):
    i = pl.program_id(2)
    nd = tq // tk
    q = (q_ref[...].astype(F32) * (SB_HEAD_DIM ** -0.5 * LOG2_E)).astype(BF16)
    tt = tt_ref[...]
    head_a = lax.broadcasted_iota(jnp.int32, (tk, 2 * SB_HEAD_DIM), 1) < SB_HEAD_DIM
    col = lax.broadcasted_iota(jnp.int32, (tk, 2 * tk), 1)
    key = jnp.where(col >= tk, col - tk, col)
    row = lax.broadcasted_iota(jnp.int32, (tk, 2 * tk), 0)
    causal_bias = jnp.where(key < row, 0.0, SB_MASKED)

    def block_diag(x):
        zero = jnp.zeros_like(x)
        return jnp.concatenate([jnp.where(head_a, x, zero), jnp.where(head_a, zero, x)], axis=0)

    def step(delta, diagonal):
        zs, v_bds = [], []
        for r in range(nd):
            jb = i * nd + r - delta
            start = pl.multiple_of(jnp.maximum(jb, 0) * tk, tk)
            k_bd = block_diag(k_ref[pl.ds(start, tk), :])
            v_bds.append(block_diag(v_ref[pl.ds(start, tk), :]))
            z = _dot_nt(q[r * tk:(r + 1) * tk, :], k_bd)
            first_key = jnp.where(jb == 0, FRONT - N_META, jnp.where(jb < 0, tk, 0))
            z = z + jnp.where(key[0:1, :] >= first_key, 0.0, SB_MASKED)
            if diagonal:
                z = z + causal_bias
            zs.append(z)
        z = jnp.concatenate(zs, axis=0)
        s = jnp.maximum(z, 0.0) + jnp.log2(1.0 + jnp.exp2(-jnp.abs(z)))
        log_beta = z - s
        sums = _dot(s.astype(BF16), tt)
        carry = c_ref[...]
        tail = sums[:, :2 * tk] + carry
        w = jnp.exp2(log_beta + tail).astype(BF16)
        carry = carry + sums[:, 2 * tk:]
        c_ref[...] = carry
        acc_ref[...] += jnp.concatenate(
            [_dot(w[r * tk:(r + 1) * tk, :], v_bds[r]) for r in range(nd)], axis=0)
        return jnp.max(jnp.max(carry, axis=0, keepdims=True), axis=1, keepdims=True)[0, 0]

    acc_ref[...] = jnp.zeros_like(acc_ref)
    c_ref[...] = jnp.zeros_like(c_ref)
    stick = step(0, True)

    def live(state):
        delta, stick = state
        return jnp.logical_and(delta < (i + 1) * nd, stick > SB_LOG2_UNDERFLOW)

    lax.while_loop(live, lambda state: (state[0] + 1, step(state[0], False)), (1, stick))
    o_ref[...] = acc_ref[...].astype(o_ref.dtype)


def _stick_breaking(proj3, tq, tk):
    b, lp, _ = proj3.shape
    pair = 2 * SB_HEAD_DIM
    n_pairs = SB_WIDTH // pair
    kern = functools.partial(_sb_kernel, tq=tq, tk=tk)
    return pl.pallas_call(
        kern,
        grid=(b, n_pairs, lp // tq),
        in_specs=[
            pl.BlockSpec((None, tq, pair), lambda bi, hp, i: (bi, i, COL_SBQ // pair + hp)),
            pl.BlockSpec((None, lp, pair), lambda bi, hp, i: (bi, 0, COL_SBK // pair + hp)),
            pl.BlockSpec((None, lp, pair), lambda bi, hp, i: (bi, 0, COL_SBV // pair + hp)),
            _resident((2 * tk, 4 * tk)),
        ],
        out_specs=pl.BlockSpec((None, tq, pair), lambda bi, hp, i: (bi, i, hp)),
        out_shape=jax.ShapeDtypeStruct((b, lp, SB_WIDTH), BF16),
        scratch_shapes=[
            pltpu.VMEM((tq, pair), F32),
            pltpu.VMEM((tq, 2 * tk), F32),
        ],
        compiler_params=pltpu.CompilerParams(
            dimension_semantics=("arbitrary", "arbitrary", "arbitrary"),
            vmem_limit_bytes=VMEM_LIMIT_BYTES),
        name="stick_breaking",
    )(proj3, proj3, proj3, _sb_tail_matrix(tk))


def _gla_constants(c, levels, mxu_levels):
    t = np.arange(c)[:, None]
    j = np.arange(c)[None, :]
    blocks = [(j <= t)]
    for l in range(mxu_levels):
        half = 1 << l
        p = (t >> (l + 1) << (l + 1)) + half
        upper = ((t >> l) & 1) == 1
        blocks.append(np.where(upper, (j >= p) & (j <= t), (j > t) & (j < p)))
    e = np.concatenate(blocks, axis=0).astype(np.float32)
    e2 = np.concatenate([e, e], axis=1)
    s = np.arange(c)[None, :]
    x = t ^ s
    lvl = np.where(x > 0, np.floor(np.log2(np.maximum(x, 1))).astype(np.int32), levels)
    lvl = np.where(s > t, -1, lvl).astype(np.int32)
    return jnp.asarray(e2, dtype=BF16), jnp.asarray(lvl)


def _gla_kernel(q_ref, k_ref, v01_ref, v23_ref, r01_ref, r23_ref, lr_ref, wup_ref, bgk_ref,
                gn_ref, e2_ref, lvl_ref, o_ref, st_ref, *, c, n_chunks, levels, mxu_levels):
    dk, dv = GLA_HEAD_DK, GLA_HEAD_DV
    heads = [slice(h * dk, (h + 1) * dk) for h in range(GLA_HEADS)]
    v_refs = (v01_ref, v01_ref, v23_ref, v23_ref)

    @pl.when(pl.program_id(1) == 0)
    def _():
        st_ref[...] = jnp.zeros_like(st_ref)

    rank_lanes = lax.broadcasted_iota(jnp.int32, (c, LANES), 1) < GATE_RANK
    pos = lax.broadcasted_iota(jnp.int32, (c, GLA_K_WIDTH), 0)
    row_bit = [((pos >> l) & 1) == 1 for l in range(mxu_levels)]

    def chunk(rows):
        lr = jnp.where(rank_lanes, lr_ref[rows, :], jnp.zeros((c, LANES), BF16))
        pre = (_dot(lr, wup_ref[...]) + bgk_ref[...]) * LOG2_E
        g = (jnp.minimum(pre, 0.0) - jnp.log2(1.0 + jnp.exp2(-jnp.abs(pre)))) * (1.0 / GATE_TAU)
        g_hi, g_lo = _split_hi_lo(g)
        x = _dot(e2_ref[...], jnp.concatenate([g_hi, g_lo], axis=0))
        cum = x[:c]

        def level_exponent(l):
            if l < mxu_levels:
                return x[(l + 1) * c:(l + 2) * c]
            half = 1 << l
            parts = []
            for base in range(0, c, 2 * half):
                p = base + half
                ref = jnp.broadcast_to(cum[p - 1:p, :], (half, cum.shape[1]))
                parts.append(ref - cum[base:p])
                parts.append(cum[p:p + half] - ref)
            return jnp.concatenate(parts, axis=0)

        q = q_ref[rows, :].astype(F32) * (dk ** -0.5)
        k = k_ref[rows, :].astype(F32)
        lvl = lvl_ref[...]

        def query_or_key(l):
            half = 1 << l
            if half >= SUBLANES:
                return jnp.concatenate(
                    [(q if (r0 // half) % 2 else k)[r0:r0 + half] for r0 in range(0, c, half)],
                    axis=0)
            return jnp.where(row_bit[l], q, k)

        q16, k16 = q.astype(BF16), k.astype(BF16)
        att = [jnp.where(lvl == levels, _dot_nt(q16[:, hs], k16[:, hs]), 0.0) for hs in heads]
        for l in range(levels):
            qk = (query_or_key(l) * jnp.exp2(level_exponent(l))).astype(BF16)
            att = [jnp.where(lvl == l, _dot_nt(qk[:, hs], qk[:, hs]), a)
                   for hs, a in zip(heads, att)]

        last = cum[c - 1:c, :]
        q_dec = (q * jnp.exp2(cum)).astype(BF16)
        k_dec = (k * jnp.exp2(last - cum)).astype(BF16)
        st_decay = jnp.exp2(last)
        outs = []
        for h, hs in enumerate(heads):
            v = v_refs[h][rows, (h % 2) * dv:(h % 2 + 1) * dv]
            st = st_ref[h]
            o = _dot(att[h].astype(BF16), v) + _dot_nt(q_dec[:, hs], st.astype(BF16))
            st_ref[h] = st * st_decay[:, hs] + lax.dot_general(
                v, k_dec[:, hs], (((0,), (0,)), ((), ())), preferred_element_type=F32)
            outs.append(_rms(o, gn_ref[...]))
        r = jnp.concatenate([r01_ref[rows, :], r23_ref[rows, :]], axis=1).astype(F32)
        o_ref[rows, :] = (jnp.concatenate(outs, axis=1) * (r * jax.nn.sigmoid(r))).astype(o_ref.dtype)

    for n in range(n_chunks):
        chunk(slice(n * c, (n + 1) * c))


def _gla(proj3, w_up_pad, b_gk, head_gain, tm):
    b, lp, _ = proj3.shape
    c, levels, mxu_levels = GLA_CHUNK, GLA_LEVELS, GLA_MXU_LEVELS
    e2, lvl = _gla_constants(c, levels, mxu_levels)
    kw, vw, vb = GLA_K_WIDTH, GLA_V_WIDTH, GLA_VBLOCK
    n_chunks = tm // c
    kern = functools.partial(_gla_kernel, c=c, n_chunks=n_chunks, levels=levels,
                             mxu_levels=mxu_levels)
    return pl.pallas_call(
        kern,
        grid=(b, lp // tm),
        in_specs=[
            pl.BlockSpec((None, tm, kw), lambda bi, t: (bi, t, COL_GQ // kw)),
            pl.BlockSpec((None, tm, kw), lambda bi, t: (bi, t, COL_GK // kw)),
            pl.BlockSpec((None, tm, vb), lambda bi, t: (bi, t, COL_GV // vb)),
            pl.BlockSpec((None, tm, vb), lambda bi, t: (bi, t, COL_GV // vb + 1)),
            pl.BlockSpec((None, tm, vb), lambda bi, t: (bi, t, COL_GR // vb)),
            pl.BlockSpec((None, tm, vb), lambda bi, t: (bi, t, COL_GR // vb + 1)),
            pl.BlockSpec((None, tm, LANES), lambda bi, t: (bi, t, COL_LR // LANES)),
            _resident((LANES, kw)),
            _resident((1, kw)),
            _resident((1, GLA_HEAD_DV)),
            _resident(((mxu_levels + 1) * c, 2 * c)),
            _resident((c, c)),
        ],
        out_specs=pl.BlockSpec((None, tm, vw), lambda bi, t: (bi, t, 0)),
        out_shape=jax.ShapeDtypeStruct((b, lp, vw), BF16),
        scratch_shapes=[pltpu.VMEM((GLA_HEADS, GLA_HEAD_DV, GLA_HEAD_DK), F32)],
        compiler_params=pltpu.CompilerParams(
            dimension_semantics=("arbitrary", "arbitrary"),
            vmem_limit_bytes=VMEM_LIMIT_BYTES),
        name="gla",
    )(proj3, proj3, proj3, proj3, proj3, proj3, proj3, w_up_pad, b_gk, head_gain, e2, lvl)


def _merge_kernel(x_ref, meta_ref, gpre_ref, wm_ref, sb_ref, og_ref, wsb_ref, wgla_ref, wo_ref,
                  gpost_ref, o_ref, h_ref, wm16_ref, *, tm):
    @pl.when(jnp.logical_and(pl.program_id(0) == 0, pl.program_id(1) == 0))
    def _():
        for c0 in range(0, 2 * D_MODEL, MXU_WIDTH):
            cols = slice(c0, c0 + MXU_WIDTH)
            wm16_ref[:, cols] = wm_ref[cols, :].T.astype(BF16)

    h = _padded_rows(x_ref, meta_ref, h_ref, tm)
    hn = _rms(h, gpre_ref[...]).astype(BF16)
    m = _dot(hn, wm16_ref[...])
    a = _dot(sb_ref[...], wsb_ref[...])
    b = _dot(og_ref[...], wgla_ref[...])
    mixed = jax.nn.sigmoid(m[:, :D_MODEL]) * a + jax.nn.sigmoid(m[:, D_MODEL:]) * b
    mix = _dot(mixed.astype(BF16), wo_ref[...])
    o_ref[...] = h + _rms(mix, gpost_ref[...])


def _merge(x, meta, g_pre, w_in_t, sb, og, w_sb, w_gla, w_o, g_post, tm, lp):
    b = x.shape[0]
    kern = functools.partial(_merge_kernel, tm=tm)
    merge_rows = pl.BlockSpec(
        (pl.Squeezed(), pl.Element(2 * D_MODEL), pl.Element(D_MODEL)),
        lambda bi, i: (0, COL_MERGE, 0), pipeline_mode=pl.Buffered(1))
    return pl.pallas_call(
        kern,
        grid=(b, lp // tm),
        in_specs=[
            _x_window(tm),
            _resident((N_META, D_MODEL)),
            _resident((1, D_MODEL)),
            merge_rows,
            pl.BlockSpec((None, tm, SB_WIDTH), lambda bi, i: (bi, i, 0)),
            pl.BlockSpec((None, tm, GLA_V_WIDTH), lambda bi, i: (bi, i, 0)),
            _resident((SB_WIDTH, D_MODEL)),
            _resident((GLA_V_WIDTH, D_MODEL)),
            _resident((D_MODEL, D_MODEL)),
            _resident((1, D_MODEL)),
        ],
        out_specs=pl.BlockSpec((None, tm, D_MODEL), lambda bi, i: (bi, i, 0)),
        out_shape=jax.ShapeDtypeStruct((b, lp, D_MODEL), F32),
        scratch_shapes=[
            pltpu.VMEM((tm, D_MODEL), F32),
            pltpu.VMEM((D_MODEL, 2 * D_MODEL), BF16),
        ],
        compiler_params=pltpu.CompilerParams(
            dimension_semantics=("arbitrary", "arbitrary"), vmem_limit_bytes=VMEM_LIMIT_BYTES),
        name="merge",
    )(x, meta, g_pre, w_in_t, sb, og, w_sb, w_gla, w_o, g_post)


def _ffn_kernel(h_ref, hist_ref, gpre_ref, wup_ref, wgate_ref, cw_ref, cb_ref, wdown_ref,
                gpost_ref, o_ref, tail_ref, act_ref, *, tm):
    chunks = [slice(c0, c0 + FF_CHUNK) for c0 in range(0, D_FF, FF_CHUNK)]

    @pl.when(pl.program_id(1) == 0)
    def _():
        hist = _rms(hist_ref[...], gpre_ref[...]).astype(BF16)
        for cols in chunks:
            tail_ref[:, cols] = _dot(hist, wup_ref[:, cols])

    h = h_ref[...]
    hn = _rms(h, gpre_ref[...]).astype(BF16)
    row = lax.broadcasted_iota(jnp.int32, (SUBLANES, FF_CHUNK), 0)
    for cols in chunks:
        up = _dot(hn, wup_ref[:, cols])
        gate = _dot(hn, wgate_ref[:, cols])
        prev = tail_ref[:, cols]
        r1 = pltpu.roll(up, 1, 0)
        r2 = pltpu.roll(up, 2, 0)
        head1 = jnp.where(row == 0, prev[7:8, :], r1[:SUBLANES])
        head2 = jnp.where(row == 0, prev[6:7, :], jnp.where(row == 1, prev[7:8, :], r2[:SUBLANES]))
        up1 = jnp.concatenate([head1, r1[SUBLANES:]], axis=0)
        up2 = jnp.concatenate([head2, r2[SUBLANES:]], axis=0)
        tail_ref[:, cols] = up[tm - SUBLANES:, :]
        cw = cw_ref[:, cols]
        y = cw[0:1, :] * up2 + cw[1:2, :] * up1 + cw[2:3, :] * up + cb_ref[:, cols]
        act_ref[:, cols] = (jax.nn.gelu(y, approximate=True) * gate).astype(BF16)
    ffn = _dot(act_ref[...], wdown_ref[...])
    o_ref[...] = h + _rms(ffn, gpost_ref[...])


def _ffn(h3, g_pre, w_up, w_gate, conv_w, conv_b, w_down, g_post, tm):
    b, lp, _ = h3.shape
    s = lp - FRONT
    kern = functools.partial(_ffn_kernel, tm=tm)

    def rows(n, start):
        return pl.BlockSpec((pl.Squeezed(), pl.Element(n), pl.Element(D_MODEL)), start)

    return pl.pallas_call(
        kern,
        grid=(b, s // tm),
        in_specs=[
            rows(tm, lambda bi, i: (bi, pl.multiple_of(FRONT + i * tm, Q_BLOCK), 0)),
            rows(SUBLANES, lambda bi, i: (bi, FRONT - SUBLANES, 0)),
            _resident((1, D_MODEL)),
            _resident((D_MODEL, D_FF)),
            _resident((D_MODEL, D_FF)),
            _resident((CONV_W, D_FF)),
            _resident((1, D_FF)),
            _resident((D_FF, D_MODEL)),
            _resident((1, D_MODEL)),
        ],
        out_specs=pl.BlockSpec((None, tm, D_MODEL), lambda bi, i: (bi, i, 0)),
        out_shape=jax.ShapeDtypeStruct((b, s, D_MODEL), F32),
        scratch_shapes=[
            pltpu.VMEM((SUBLANES, D_FF), F32),
            pltpu.VMEM((tm, D_FF), BF16),
        ],
        compiler_params=pltpu.CompilerParams(
            dimension_semantics=("arbitrary", "arbitrary"), vmem_limit_bytes=VMEM_LIMIT_BYTES),
        name="ffn",
    )(h3, h3, g_pre, w_up, w_gate, conv_w, conv_b, w_down, g_post)


def _pick_tile(n, want):
    t = want
    while n % t:
        t -= Q_BLOCK
    return t


def kernel(x, meta_tokens, norm_mix_pre, w_in, w_gk_up, b_gk, gla_head_norm, w_sb_out, w_gla_out, w_o,
           norm_mix_post, norm_ffn_pre, w_ffn_up, w_ffn_gate, conv_w, conv_b, w_ffn_down, norm_ffn_post):
    b, s, _ = x.shape
    assert s % Q_BLOCK == 0 and norm_mix_pre.shape[0] == 1
    lp = FRONT + s
    tm = _pick_tile(lp, ROW_TILE)
    tq = _pick_tile(lp, SB_TQ)
    tf = _pick_tile(s, FFN_TILE)
    d = 0

    w_in_t = jnp.swapaxes(w_in, 1, 2)
    w_up_pad = jnp.pad(w_gk_up[d].astype(BF16), ((0, LANES - GATE_RANK), (0, 0)))

    proj3 = _inproj(x, meta_tokens, norm_mix_pre[d][None, :], w_in_t, tm, lp)
    sb = _stick_breaking(proj3, tq, SB_TK)
    og = _gla(proj3, w_up_pad, b_gk[d][None, :], gla_head_norm[d][None, :], tm)
    h1 = _merge(x, meta_tokens, norm_mix_pre[d][None, :], w_in_t, sb, og,
                w_sb_out[d].astype(BF16), w_gla_out[d].astype(BF16), w_o[d].astype(BF16),
                norm_mix_post[d][None, :], tm, lp)
    return _ffn(h1, norm_ffn_pre[d][None, :], w_ffn_up[d].astype(BF16), w_ffn_gate[d].astype(BF16),
                conv_w[d], conv_b[d][None, :], w_ffn_down[d].astype(BF16),
                norm_ffn_post[d][None, :], tf)
```

```python
import functools

import numpy as np
import jax
import jax.numpy as jnp
from jax import lax
from jax.experimental import pallas as pl
from jax.experimental.pallas import tpu as pltpu

F32 = jnp.float32
BF16 = jnp.bfloat16

D_MODEL = 1024
N_META = 16
Q_BLOCK = 128
SB_HEADS = 8
SB_HEAD_DIM = 64
GLA_HEADS = 4
GLA_HEAD_DK = 128
GLA_HEAD_DV = 256
GATE_RANK = 16
GATE_TAU = 16.0
D_FF = 2816
CONV_W = 3
EPS = 1e-6

SB_WIDTH = SB_HEADS * SB_HEAD_DIM
GLA_K_WIDTH = GLA_HEADS * GLA_HEAD_DK
GLA_V_WIDTH = GLA_HEADS * GLA_HEAD_DV

LANES = 128
SUBLANES = 8
MXU_WIDTH = 256
VMEM_LIMIT_BYTES = 56 * 1024 * 1024

COL_SBQ = 0
COL_SBK = COL_SBQ + SB_WIDTH
COL_SBV = COL_SBK + SB_WIDTH
COL_GQ = COL_SBV + SB_WIDTH
COL_GK = COL_GQ + GLA_K_WIDTH
COL_GV = COL_GK + GLA_K_WIDTH
COL_GR = COL_GV + GLA_V_WIDTH
COL_LR = COL_GR + GLA_V_WIDTH
COL_MERGE = COL_LR + GATE_RANK
PROJ_WIDTH = COL_LR + LANES

FRONT = Q_BLOCK
ROW_TILE = 640
FFN_TILE = 1024
SB_TQ = 1664
SB_TK = 128
GLA_TILE = 1664
GLA_CHUNK = 128
GLA_LEVELS = 7
GLA_MXU_LEVELS = 3
GLA_VBLOCK = 512
FF_CHUNK = 256
LOG2_E = 1.4426950408889634
SB_LOG2_UNDERFLOW = -126.0
SB_MASKED = -1e30


def _dot(a, b):
    return jnp.dot(a, b, preferred_element_type=F32)


def _dot_nt(a, b):
    return lax.dot_general(a, b, (((1,), (1,)), ((), ())), preferred_element_type=F32)


def _rms(x, gain):
    ms = jnp.mean(x * x, axis=-1, keepdims=True)
    return x * lax.rsqrt(ms + EPS) * gain


def _split_hi_lo(x):
    hi = x.astype(BF16)
    lo = (x - hi.astype(F32)).astype(BF16)
    return hi, lo


def _resident(shape):
    nd = len(shape)
    return pl.BlockSpec(shape, lambda *_: (0,) * nd, pipeline_mode=pl.Buffered(1))


def _x_window(tm):
    return pl.BlockSpec(
        (pl.Squeezed(), pl.Element(tm), pl.Element(D_MODEL)),
        lambda b, i: (b, pl.multiple_of(jnp.maximum(i * tm - FRONT, 0), Q_BLOCK), 0))


def _padded_rows(x_ref, meta_ref, h_ref, tm):
    i = pl.program_id(1)

    @pl.when(i == 0)
    def _():
        h_ref[0:FRONT - N_META, :] = jnp.zeros((FRONT - N_META, D_MODEL), F32)
        h_ref[FRONT - N_META:FRONT, :] = meta_ref[...]
        h_ref[FRONT:, :] = x_ref[0:tm - FRONT, :]

    @pl.when(i > 0)
    def _():
        h_ref[...] = x_ref[...]

    return h_ref[...]


def _inproj_kernel(x_ref, meta_ref, g_ref, w_ref, o_ref, h_ref, w16_ref, *, tm):
    chunks = [slice(c0, min(c0 + MXU_WIDTH, PROJ_WIDTH)) for c0 in range(0, PROJ_WIDTH, MXU_WIDTH)]

    @pl.when(jnp.logical_and(pl.program_id(0) == 0, pl.program_id(1) == 0))
    def _():
        for cols in chunks:
            w16_ref[:, cols] = w_ref[cols, :].T.astype(BF16)

    h = _padded_rows(x_ref, meta_ref, h_ref, tm)
    hn = _rms(h, g_ref[...]).astype(BF16)
    for cols in chunks:
        o_ref[:, cols] = _dot(hn, w16_ref[:, cols]).astype(BF16)


def _inproj(x, meta, gain, w_in_t, tm, lp):
    b = x.shape[0]
    kern = functools.partial(_inproj_kernel, tm=tm)
    return pl.pallas_call(
        kern,
        grid=(b, lp // tm),
        in_specs=[
            _x_window(tm),
            _resident((N_META, D_MODEL)),
            _resident((1, D_MODEL)),
            pl.BlockSpec((None, PROJ_WIDTH, D_MODEL), lambda bi, i: (0, 0, 0),
                         pipeline_mode=pl.Buffered(1)),
        ],
        out_specs=pl.BlockSpec((None, tm, PROJ_WIDTH), lambda bi, i: (bi, i, 0)),
        out_shape=jax.ShapeDtypeStruct((b, lp, PROJ_WIDTH), BF16),
        scratch_shapes=[
            pltpu.VMEM((tm, D_MODEL), F32),
            pltpu.VMEM((D_MODEL, PROJ_WIDTH), BF16),
        ],
        compiler_params=pltpu.CompilerParams(
            dimension_semantics=("arbitrary", "arbitrary"), vmem_limit_bytes=VMEM_LIMIT_BYTES),
        name="inproj",
    )(x, meta, gain, w_in_t)


def _sb_tail_matrix(tk):
    j = np.arange(tk)
    later = (j[:, None] > j[None, :]).astype(np.float32)
    ones, zero = np.ones((tk, tk), np.float32), np.zeros((tk, tk), np.float32)
    return jnp.asarray(-np.block([[later, zero, ones, zero], [zero, later, zero, ones]]), dtype=BF16)


def _sb_kernel(q_ref, k_ref, v_ref, tt_ref, o_ref, acc_ref, c_ref, *, tq, tk):
    i = pl.program_id(2)
    nd = tq // tk
    q = (q_ref[...].astype(F32) * (SB_HEAD_DIM ** -0.5 * LOG2_E)).astype(BF16)
    tt = tt_ref[...]
    head_a = lax.broadcasted_iota(jnp.int32, (tk, 2 * SB_HEAD_DIM), 1) < SB_HEAD_DIM
    col = lax.broadcasted_iota(jnp.int32, (tk, 2 * tk), 1)
    key = jnp.where(col >= tk, col - tk, col)
    row = lax.broadcasted_iota(jnp.int32, (tk, 2 * tk), 0)
    causal_bias = jnp.where(key < row, 0.0, SB_MASKED)

    def block_diag(x):
        zero = jnp.zeros_like(x)
        return jnp.concatenate([jnp.where(head_a, x, zero), jnp.where(head_a, zero, x)], axis=0)

    def step(delta, diagonal):
        zs, v_bds = [], []
        for r in range(nd):
            jb = i * nd + r - delta
            start = pl.multiple_of(jnp.maximum(jb, 0) * tk, tk)
            k_bd = block_diag(k_ref[pl.ds(start, tk), :])
            v_bds.append(block_diag(v_ref[pl.ds(start, tk), :]))
            z = _dot_nt(q[r * tk:(r + 1) * tk, :], k_bd)
            first_key = jnp.where(jb == 0, FRONT - N_META, jnp.where(jb < 0, tk, 0))
            z = z + jnp.where(key[0:1, :] >= first_key, 0.0, SB_MASKED)
            if diagonal:
                z = z + causal_bias
            zs.append(z)
        z = jnp.concatenate(zs, axis=0)
        s = jnp.maximum(z, 0.0) + jnp.log2(1.0 + jnp.exp2(-jnp.abs(z)))
        log_beta = z - s
        sums = _dot(s.astype(BF16), tt)
        carry = c_ref[...]
        tail = sums[:, :2 * tk] + carry
        w = jnp.exp2(log_beta + tail).astype(BF16)
        carry = carry + sums[:, 2 * tk:]
        c_ref[...] = carry
        acc_ref[...] += jnp.concatenate(
            [_dot(w[r * tk:(r + 1) * tk, :], v_bds[r]) for r in range(nd)], axis=0)
        return jnp.max(jnp.max(carry, axis=0, keepdims=True), axis=1, keepdims=True)[0, 0]

    acc_ref[...] = jnp.zeros_like(acc_ref)
    c_ref[...] = jnp.zeros_like(c_ref)
    stick = step(0, True)

    def live(state):
        delta, stick = state
        return jnp.logical_and(delta < (i + 1) * nd, stick > SB_LOG2_UNDERFLOW)

    lax.while_loop(live, lambda state: (state[0] + 1, step(state[0], False)), (1, stick))
    o_ref[...] = acc_ref[...].astype(o_ref.dtype)


def _stick_breaking(proj3, tq, tk):
    b, lp, _ = proj3.shape
    pair = 2 * SB_HEAD_DIM
    n_pairs = SB_WIDTH // pair
    kern = functools.partial(_sb_kernel, tq=tq, tk=tk)
    return pl.pallas_call(
        kern,
        grid=(b, n_pairs, lp // tq),
        in_specs=[
            pl.BlockSpec((None, tq, pair), lambda bi, hp, i: (bi, i, COL_SBQ // pair + hp)),
            pl.BlockSpec((None, lp, pair), lambda bi, hp, i: (bi, 0, COL_SBK // pair + hp)),
            pl.BlockSpec((None, lp, pair), lambda bi, hp, i: (bi, 0, COL_SBV // pair + hp)),
            _resident((2 * tk, 4 * tk)),
        ],
        out_specs=pl.BlockSpec((None, tq, pair), lambda bi, hp, i: (bi, i, hp)),
        out_shape=jax.ShapeDtypeStruct((b, lp, SB_WIDTH), BF16),
        scratch_shapes=[
            pltpu.VMEM((tq, pair), F32),
            pltpu.VMEM((tq, 2 * tk), F32),
        ],
        compiler_params=pltpu.CompilerParams(
            dimension_semantics=("arbitrary", "arbitrary", "arbitrary"),
            vmem_limit_bytes=VMEM_LIMIT_BYTES),
        name="stick_breaking",
    )(proj3, proj3, proj3, _sb_tail_matrix(tk))


def _gla_constants(c, levels, mxu_levels):
    t = np.arange(c)[:, None]
    j = np.arange(c)[None, :]
    blocks = [(j <= t)]
    for l in range(mxu_levels):
        half = 1 << l
        p = (t >> (l + 1) << (l + 1)) + half
        upper = ((t >> l) & 1) == 1
        blocks.append(np.where(upper, (j >= p) & (j <= t), (j > t) & (j < p)))
    e = np.concatenate(blocks, axis=0).astype(np.float32)
    e2 = np.concatenate([e, e], axis=1)
    s = np.arange(c)[None, :]
    x = t ^ s
    lvl = np.where(x > 0, np.floor(np.log2(np.maximum(x, 1))).astype(np.int32), levels)
    lvl = np.where(s > t, -1, lvl).astype(np.int32)
    return jnp.asarray(e2, dtype=BF16), jnp.asarray(lvl)


def _gla_kernel(q_ref, k_ref, v01_ref, v23_ref, r01_ref, r23_ref, lr_ref, wup_ref, bgk_ref,
                gn_ref, e2_ref, lvl_ref, o_ref, st_ref, *, c, n_chunks, levels, mxu_levels):
    dk, dv = GLA_HEAD_DK, GLA_HEAD_DV
    heads = [slice(h * dk, (h + 1) * dk) for h in range(GLA_HEADS)]
    v_refs = (v01_ref, v01_ref, v23_ref, v23_ref)

    @pl.when(pl.program_id(1) == 0)
    def _():
        st_ref[...] = jnp.zeros_like(st_ref)

    rank_lanes = lax.broadcasted_iota(jnp.int32, (c, LANES), 1) < GATE_RANK
    pos = lax.broadcasted_iota(jnp.int32, (c, GLA_K_WIDTH), 0)
    row_bit = [((pos >> l) & 1) == 1 for l in range(mxu_levels)]

    def chunk(rows):
        lr = jnp.where(rank_lanes, lr_ref[rows, :], jnp.zeros((c, LANES), BF16))
        pre = (_dot(lr, wup_ref[...]) + bgk_ref[...]) * LOG2_E
        g = (jnp.minimum(pre, 0.0) - jnp.log2(1.0 + jnp.exp2(-jnp.abs(pre)))) * (1.0 / GATE_TAU)
        g_hi, g_lo = _split_hi_lo(g)
        x = _dot(e2_ref[...], jnp.concatenate([g_hi, g_lo], axis=0))
        cum = x[:c]

        def level_exponent(l):
            if l < mxu_levels:
                return x[(l + 1) * c:(l + 2) * c]
            half = 1 << l
            parts = []
            for base in range(0, c, 2 * half):
                p = base + half
                ref = jnp.broadcast_to(cum[p - 1:p, :], (half, cum.shape[1]))
                parts.append(ref - cum[base:p])
                parts.append(cum[p:p + half] - ref)
            return jnp.concatenate(parts, axis=0)

        q = q_ref[rows, :].astype(F32) * (dk ** -0.5)
        k = k_ref[rows, :].astype(F32)
        lvl = lvl_ref[...]

        def query_or_key(l):
            half = 1 << l
            if half >= SUBLANES:
                return jnp.concatenate(
                    [(q if (r0 // half) % 2 else k)[r0:r0 + half] for r0 in range(0, c, half)],
                    axis=0)
            return jnp.where(row_bit[l], q, k)

        q16, k16 = q.astype(BF16), k.astype(BF16)
        att = [jnp.where(lvl == levels, _dot_nt(q16[:, hs], k16[:, hs]), 0.0) for hs in heads]
        for l in range(levels):
            qk = (query_or_key(l) * jnp.exp2(level_exponent(l))).astype(BF16)
            att = [jnp.where(lvl == l, _dot_nt(qk[:, hs], qk[:, hs]), a)
                   for hs, a in zip(heads, att)]

        last = cum[c - 1:c, :]
        q_dec = (q * jnp.exp2(cum)).astype(BF16)
        k_dec = (k * jnp.exp2(last - cum)).astype(BF16)
        st_decay = jnp.exp2(last)
        outs = []
        for h, hs in enumerate(heads):
            v = v_refs[h][rows, (h % 2) * dv:(h % 2 + 1) * dv]
            st = st_ref[h]
            o = _dot(att[h].astype(BF16), v) + _dot_nt(q_dec[:, hs], st.astype(BF16))
            st_ref[h] = st * st_decay[:, hs] + lax.dot_general(
                v, k_dec[:, hs], (((0,), (0,)), ((), ())), preferred_element_type=F32)
            outs.append(_rms(o, gn_ref[...]))
        r = jnp.concatenate([r01_ref[rows, :], r23_ref[rows, :]], axis=1).astype(F32)
        o_ref[rows, :] = (jnp.concatenate(outs, axis=1) * (r * jax.nn.sigmoid(r))).astype(o_ref.dtype)

    for n in range(n_chunks):
        chunk(slice(n * c, (n + 1) * c))


def _gla(proj3, w_up_pad, b_gk, head_gain, tm):
    b, lp, _ = proj3.shape
    c, levels, mxu_levels = GLA_CHUNK, GLA_LEVELS, GLA_MXU_LEVELS
    e2, lvl = _gla_constants(c, levels, mxu_levels)
    kw, vw, vb = GLA_K_WIDTH, GLA_V_WIDTH, GLA_VBLOCK
    n_chunks = tm // c
    kern = functools.partial(_gla_kernel, c=c, n_chunks=n_chunks, levels=levels,
                             mxu_levels=mxu_levels)
    return pl.pallas_call(
        kern,
        grid=(b, lp // tm),
        in_specs=[
            pl.BlockSpec((None, tm, kw), lambda bi, t: (bi, t, COL_GQ // kw)),
            pl.BlockSpec((None, tm, kw), lambda bi, t: (bi, t, COL_GK // kw)),
            pl.BlockSpec((None, tm, vb), lambda bi, t: (bi, t, COL_GV // vb)),
            pl.BlockSpec((None, tm, vb), lambda bi, t: (bi, t, COL_GV // vb + 1)),
            pl.BlockSpec((None, tm, vb), lambda bi, t: (bi, t, COL_GR // vb)),
            pl.BlockSpec((None, tm, vb), lambda bi, t: (bi, t, COL_GR // vb + 1)),
            pl.BlockSpec((None, tm, LANES), lambda bi, t: (bi, t, COL_LR // LANES)),
            _resident((LANES, kw)),
            _resident((1, kw)),
            _resident((1, GLA_HEAD_DV)),
            _resident(((mxu_levels + 1) * c, 2 * c)),
            _resident((c, c)),
        ],
        out_specs=pl.BlockSpec((None, tm, vw), lambda bi, t: (bi, t, 0)),
        out_shape=jax.ShapeDtypeStruct((b, lp, vw), BF16),
        scratch_shapes=[pltpu.VMEM((GLA_HEADS, GLA_HEAD_DV, GLA_HEAD_DK), F32)],
        compiler_params=pltpu.CompilerParams(
            dimension_semantics=("arbitrary", "arbitrary"),
            vmem_limit_bytes=VMEM_LIMIT_BYTES),
        name="gla",
    )(proj3, proj3, proj3, proj3, proj3, proj3, proj3, w_up_pad, b_gk, head_gain, e2, lvl)


def _merge_kernel(x_ref, meta_ref, gpre_ref, wm_ref, sb_ref, og_ref, wsb_ref, wgla_ref, wo_ref,
                  gpost_ref, o_ref, h_ref, wm16_ref, *, tm):
    @pl.when(jnp.logical_and(pl.program_id(0) == 0, pl.program_id(1) == 0))
    def _():
        for c0 in range(0, 2 * D_MODEL, MXU_WIDTH):
            cols = slice(c0, c0 + MXU_WIDTH)
            wm16_ref[:, cols] = wm_ref[cols, :].T.astype(BF16)

    h = _padded_rows(x_ref, meta_ref, h_ref, tm)
    hn = _rms(h, gpre_ref[...]).astype(BF16)
    m = _dot(hn, wm16_ref[...])
    a = _dot(sb_ref[...], wsb_ref[...])
    b = _dot(og_ref[...], wgla_ref[...])
    mixed = jax.nn.sigmoid(m[:, :D_MODEL]) * a + jax.nn.sigmoid(m[:, D_MODEL:]) * b
    mix = _dot(mixed.astype(BF16), wo_ref[...])
    o_ref[...] = h + _rms(mix, gpost_ref[...])


def _merge(x, meta, g_pre, w_in_t, sb, og, w_sb, w_gla, w_o, g_post, tm, lp):
    b = x.shape[0]
    kern = functools.partial(_merge_kernel, tm=tm)
    merge_rows = pl.BlockSpec(
        (pl.Squeezed(), pl.Element(2 * D_MODEL), pl.Element(D_MODEL)),
        lambda bi, i: (0, COL_MERGE, 0), pipeline_mode=pl.Buffered(1))
    return pl.pallas_call(
        kern,
        grid=(b, lp // tm),
        in_specs=[
            _x_window(tm),
            _resident((N_META, D_MODEL)),
            _resident((1, D_MODEL)),
            merge_rows,
            pl.BlockSpec((None, tm, SB_WIDTH), lambda bi, i: (bi, i, 0)),
            pl.BlockSpec((None, tm, GLA_V_WIDTH), lambda bi, i: (bi, i, 0)),
            _resident((SB_WIDTH, D_MODEL)),
            _resident((GLA_V_WIDTH, D_MODEL)),
            _resident((D_MODEL, D_MODEL)),
            _resident((1, D_MODEL)),
        ],
        out_specs=pl.BlockSpec((None, tm, D_MODEL), lambda bi, i: (bi, i, 0)),
        out_shape=jax.ShapeDtypeStruct((b, lp, D_MODEL), F32),
        scratch_shapes=[
            pltpu.VMEM((tm, D_MODEL), F32),
            pltpu.VMEM((D_MODEL, 2 * D_MODEL), BF16),
        ],
        compiler_params=pltpu.CompilerParams(
            dimension_semantics=("arbitrary", "arbitrary"), vmem_limit_bytes=VMEM_LIMIT_BYTES),
        name="merge",
    )(x, meta, g_pre, w_in_t, sb, og, w_sb, w_gla, w_o, g_post)


def _ffn_kernel(h_ref, hist_ref, gpre_ref, wup_ref, wgate_ref, cw_ref, cb_ref, wdown_ref,
                gpost_ref, o_ref, tail_ref, act_ref, *, tm):
    chunks = [slice(c0, c0 + FF_CHUNK) for c0 in range(0, D_FF, FF_CHUNK)]

    @pl.when(pl.program_id(1) == 0)
    def _():
        hist = _rms(hist_ref[...], gpre_ref[...]).astype(BF16)
        for cols in chunks:
            tail_ref[:, cols] = _dot(hist, wup_ref[:, cols])

    h = h_ref[...]
    hn = _rms(h, gpre_ref[...]).astype(BF16)
    row = lax.broadcasted_iota(jnp.int32, (SUBLANES, FF_CHUNK), 0)
    for cols in chunks:
        up = _dot(hn, wup_ref[:, cols])
        gate = _dot(hn, wgate_ref[:, cols])
        prev = tail_ref[:, cols]
        r1 = pltpu.roll(up, 1, 0)
        r2 = pltpu.roll(up, 2, 0)
        head1 = jnp.where(row == 0, prev[7:8, :], r1[:SUBLANES])
        head2 = jnp.where(row == 0, prev[6:7, :], jnp.where(row == 1, prev[7:8, :], r2[:SUBLANES]))
        up1 = jnp.concatenate([head1, r1[SUBLANES:]], axis=0)
        up2 = jnp.concatenate([head2, r2[SUBLANES:]], axis=0)
        tail_ref[:, cols] = up[tm - SUBLANES:, :]
        cw = cw_ref[:, cols]
        y = cw[0:1, :] * up2 + cw[1:2, :] * up1 + cw[2:3, :] * up + cb_ref[:, cols]
        act_ref[:, cols] = (jax.nn.gelu(y, approximate=True) * gate).astype(BF16)
    ffn = _dot(act_ref[...], wdown_ref[...])
    o_ref[...] = h + _rms(ffn, gpost_ref[...])


def _ffn(h3, g_pre, w_up, w_gate, conv_w, conv_b, w_down, g_post, tm):
    b, lp, _ = h3.shape
    s = lp - FRONT
    kern = functools.partial(_ffn_kernel, tm=tm)

    def rows(n, start):
        return pl.BlockSpec((pl.Squeezed(), pl.Element(n), pl.Element(D_MODEL)), start)

    return pl.pallas_call(
        kern,
        grid=(b, s // tm),
        in_specs=[
            rows(tm, lambda bi, i: (bi, pl.multiple_of(FRONT + i * tm, Q_BLOCK), 0)),
            rows(SUBLANES, lambda bi, i: (bi, FRONT - SUBLANES, 0)),
            _resident((1, D_MODEL)),
            _resident((D_MODEL, D_FF)),
            _resident((D_MODEL, D_FF)),
            _resident((CONV_W, D_FF)),
            _resident((1, D_FF)),
            _resident((D_FF, D_MODEL)),
            _resident((1, D_MODEL)),
        ],
        out_specs=pl.BlockSpec((None, tm, D_MODEL), lambda bi, i: (bi, i, 0)),
        out_shape=jax.ShapeDtypeStruct((b, s, D_MODEL), F32),
        scratch_shapes=[
            pltpu.VMEM((SUBLANES, D_FF), F32),
            pltpu.VMEM((tm, D_FF), BF16),
        ],
        compiler_params=pltpu.CompilerParams(
            dimension_semantics=("arbitrary", "arbitrary"), vmem_limit_bytes=VMEM_LIMIT_BYTES),
        name="ffn",
    )(h3, h3, g_pre, w_up, w_gate, conv_w, conv_b, w_down, g_post)


def _pick_tile(n, want):
    t = want
    while n % t:
        t -= Q_BLOCK
    return t


def kernel(x, meta_tokens, norm_mix_pre, w_in, w_gk_up, b_gk, gla_head_norm, w_sb_out, w_gla_out, w_o,
           norm_mix_post, norm_ffn_pre, w_ffn_up, w_ffn_gate, conv_w, conv_b, w_ffn_down, norm_ffn_post):
    b, s, _ = x.shape
    assert s % Q_BLOCK == 0 and norm_mix_pre.shape[0] == 1
    lp = FRONT + s
    tm = _pick_tile(lp, ROW_TILE)
    tq = _pick_tile(lp, SB_TQ)
    tf = _pick_tile(s, FFN_TILE)
    d = 0

    w_in_t = jnp.swapaxes(w_in, 1, 2)
    w_up_pad = jnp.pad(w_gk_up[d].astype(BF16), ((0, LANES - GATE_RANK), (0, 0)))

    proj3 = _inproj(x, meta_tokens, norm_mix_pre[d][None, :], w_in_t, tm, lp)
    sb = _stick_breaking(proj3, tq, SB_TK)
    og = _gla(proj3, w_up_pad, b_gk[d][None, :], gla_head_norm[d][None, :],
              _pick_tile(lp, GLA_TILE))
    h1 = _merge(x, meta_tokens, norm_mix_pre[d][None, :], w_in_t, sb, og,
                w_sb_out[d].astype(BF16), w_gla_out[d].astype(BF16), w_o[d].astype(BF16),
                norm_mix_post[d][None, :], tm, lp)
    return _ffn(h1, norm_ffn_pre[d][None, :], w_ffn_up[d].astype(BF16), w_ffn_gate[d].astype(BF16),
                conv_w[d], conv_b[d][None, :], w_ffn_down[d].astype(BF16),
                norm_ffn_post[d][None, :], tf)
```

```python
import functools

import numpy as np
import jax
import jax.numpy as jnp
from jax import lax
from jax.experimental import pallas as pl
from jax.experimental.pallas import tpu as pltpu

F32 = jnp.float32
BF16 = jnp.bfloat16

D_MODEL = 1024
N_META = 16
Q_BLOCK = 128
SB_HEADS = 8
SB_HEAD_DIM = 64
GLA_HEADS = 4
GLA_HEAD_DK = 128
GLA_HEAD_DV = 256
GATE_RANK = 16
GATE_TAU = 16.0
D_FF = 2816
CONV_W = 3
EPS = 1e-6

SB_WIDTH = SB_HEADS * SB_HEAD_DIM
GLA_K_WIDTH = GLA_HEADS * GLA_HEAD_DK
GLA_V_WIDTH = GLA_HEADS * GLA_HEAD_DV

LANES = 128
SUBLANES = 8
MXU_WIDTH = 256
VMEM_LIMIT_BYTES = 56 * 1024 * 1024

COL_SBQ = 0
COL_SBK = COL_SBQ + SB_WIDTH
COL_SBV = COL_SBK + SB_WIDTH
COL_GQ = COL_SBV + SB_WIDTH
COL_GK = COL_GQ + GLA_K_WIDTH
COL_GV = COL_GK + GLA_K_WIDTH
COL_GR = COL_GV + GLA_V_WIDTH
COL_LR = COL_GR + GLA_V_WIDTH
COL_MERGE = COL_LR + GATE_RANK
PROJ_WIDTH = COL_LR + LANES

FRONT = Q_BLOCK
ROW_TILE = 640
FFN_TILE = 1024
SB_TQ = 640
SB_TK = 128
GLA_CHUNK = 128
GLA_LEVELS = 7
GLA_MXU_LEVELS = 3
GLA_VBLOCK = 512
FF_CHUNK = 256
LOG2_E = 1.4426950408889634
SB_LOG2_UNDERFLOW = -126.0
SB_MASKED = -1e30


def _dot(a, b):
    return jnp.dot(a, b, preferred_element_type=F32)


def _dot_nt(a, b):
    return lax.dot_general(a, b, (((1,), (1,)), ((), ())), preferred_element_type=F32)


def _rms(x, gain):
    ms = jnp.mean(x * x, axis=-1, keepdims=True)
    return x * lax.rsqrt(ms + EPS) * gain


def _split_hi_lo(x):
    hi = x.astype(BF16)
    lo = (x - hi.astype(F32)).astype(BF16)
    return hi, lo


def _resident(shape):
    nd = len(shape)
    return pl.BlockSpec(shape, lambda *_: (0,) * nd, pipeline_mode=pl.Buffered(1))


def _x_window(tm):
    return pl.BlockSpec(
        (pl.Squeezed(), pl.Element(tm), pl.Element(D_MODEL)),
        lambda b, i: (b, pl.multiple_of(jnp.maximum(i * tm - FRONT, 0), Q_BLOCK), 0))


def _padded_rows(x_ref, meta_ref, h_ref, tm):
    i = pl.program_id(1)

    @pl.when(i == 0)
    def _():
        h_ref[0:FRONT - N_META, :] = jnp.zeros((FRONT - N_META, D_MODEL), F32)
        h_ref[FRONT - N_META:FRONT, :] = meta_ref[...]
        h_ref[FRONT:, :] = x_ref[0:tm - FRONT, :]

    @pl.when(i > 0)
    def _():
        h_ref[...] = x_ref[...]

    return h_ref[...]


def _inproj_kernel(x_ref, meta_ref, g_ref, w_ref, o_ref, h_ref, w16_ref, *, tm):
    chunks = [slice(c0, min(c0 + MXU_WIDTH, PROJ_WIDTH)) for c0 in range(0, PROJ_WIDTH, MXU_WIDTH)]

    @pl.when(jnp.logical_and(pl.program_id(0) == 0, pl.program_id(1) == 0))
    def _():
        for cols in chunks:
            w16_ref[:, cols] = w_ref[cols, :].T.astype(BF16)

    h = _padded_rows(x_ref, meta_ref, h_ref, tm)
    hn = _rms(h, g_ref[...]).astype(BF16)
    for cols in chunks:
        o_ref[:, cols] = _dot(hn, w16_ref[:, cols]).astype(BF16)


def _inproj(x, meta, gain, w_in_t, tm, lp):
    b = x.shape[0]
    kern = functools.partial(_inproj_kernel, tm=tm)
    return pl.pallas_call(
        kern,
        grid=(b, lp // tm),
        in_specs=[
            _x_window(tm),
            _resident((N_META, D_MODEL)),
            _resident((1, D_MODEL)),
            pl.BlockSpec((None, PROJ_WIDTH, D_MODEL), lambda bi, i: (0, 0, 0),
                         pipeline_mode=pl.Buffered(1)),
        ],
        out_specs=pl.BlockSpec((None, tm, PROJ_WIDTH), lambda bi, i: (bi, i, 0)),
        out_shape=jax.ShapeDtypeStruct((b, lp, PROJ_WIDTH), BF16),
        scratch_shapes=[
            pltpu.VMEM((tm, D_MODEL), F32),
            pltpu.VMEM((D_MODEL, PROJ_WIDTH), BF16),
        ],
        compiler_params=pltpu.CompilerParams(
            dimension_semantics=("arbitrary", "arbitrary"), vmem_limit_bytes=VMEM_LIMIT_BYTES),
        name="inproj",
    )(x, meta, gain, w_in_t)


def _sb_tail_matrix(tk):
    j = np.arange(tk)
    later = (j[:, None] > j[None, :]).astype(np.float32)
    ones, zero = np.ones((tk, tk), np.float32), np.zeros((tk, tk), np.float32)
    return jnp.asarray(-np.block([[later, zero, ones, zero], [zero, later, zero, ones]]), dtype=BF16)


def _sb_kernel(q_ref, k_ref, v_ref, tt_ref, o_ref, acc_ref, c_ref, *, tq, tk):
    i = pl.program_id(2)
    nd = tq // tk
    q = (q_ref[...].astype(F32) * (SB_HEAD_DIM ** -0.5 * LOG2_E)).astype(BF16)
    tt = tt_ref[...]
    head_a = lax.broadcasted_iota(jnp.int32, (tk, 2 * SB_HEAD_DIM), 1) < SB_HEAD_DIM
    col = lax.broadcasted_iota(jnp.int32, (tk, 2 * tk), 1)
    key = jnp.where(col >= tk, col - tk, col)
    row = lax.broadcasted_iota(jnp.int32, (tk, 2 * tk), 0)
    causal_bias = jnp.where(key < row, 0.0, SB_MASKED)

    def block_diag(x):
        zero = jnp.zeros_like(x)
        return jnp.concatenate([jnp.where(head_a, x, zero), jnp.where(head_a, zero, x)], axis=0)

    def step(delta, diagonal):
        zs, v_bds = [], []
        for r in range(nd):
            jb = i * nd + r - delta
            start = pl.multiple_of(jnp.maximum(jb, 0) * tk, tk)
            k_bd = block_diag(k_ref[pl.ds(start, tk), :])
            v_bds.append(block_diag(v_ref[pl.ds(start, tk), :]))
            z = _dot_nt(q[r * tk:(r + 1) * tk, :], k_bd)
            first_key = jnp.where(jb == 0, FRONT - N_META, jnp.where(jb < 0, tk, 0))
            z = z + jnp.where(key[0:1, :] >= first_key, 0.0, SB_MASKED)
            if diagonal:
                z = z + causal_bias
            zs.append(z)
        z = jnp.concatenate(zs, axis=0)
        s = jnp.maximum(z, 0.0) + jnp.log2(1.0 + jnp.exp2(-jnp.abs(z)))
        log_beta = z - s
        sums = _dot(s.astype(BF16), tt)
        carry = c_ref[...]
        tail = sums[:, :2 * tk] + carry
        w = jnp.exp2(log_beta + tail).astype(BF16)
        carry = carry + sums[:, 2 * tk:]
        c_ref[...] = carry
        acc_ref[...] += jnp.concatenate(
            [_dot(w[r * tk:(r + 1) * tk, :], v_bds[r]) for r in range(nd)], axis=0)
        return jnp.max(jnp.max(carry, axis=0, keepdims=True), axis=1, keepdims=True)[0, 0]

    acc_ref[...] = jnp.zeros_like(acc_ref)
    c_ref[...] = jnp.zeros_like(c_ref)
    step(0, True)
    stick = step(1, False)

    def live(state):
        delta, stick = state
        return jnp.logical_and(delta < (i + 1) * nd, stick > SB_LOG2_UNDERFLOW)

    lax.while_loop(live, lambda state: (state[0] + 1, step(state[0], False)), (2, stick))
    o_ref[...] = acc_ref[...].astype(o_ref.dtype)


def _stick_breaking(proj3, tq, tk):
    b, lp, _ = proj3.shape
    pair = 2 * SB_HEAD_DIM
    n_pairs = SB_WIDTH // pair
    kern = functools.partial(_sb_kernel, tq=tq, tk=tk)
    return pl.pallas_call(
        kern,
        grid=(b, n_pairs, lp // tq),
        in_specs=[
            pl.BlockSpec((None, tq, pair), lambda bi, hp, i: (bi, i, COL_SBQ // pair + hp)),
            pl.BlockSpec((None, lp, pair), lambda bi, hp, i: (bi, 0, COL_SBK // pair + hp)),
            pl.BlockSpec((None, lp, pair), lambda bi, hp, i: (bi, 0, COL_SBV // pair + hp)),
            _resident((2 * tk, 4 * tk)),
        ],
        out_specs=pl.BlockSpec((None, tq, pair), lambda bi, hp, i: (bi, i, hp)),
        out_shape=jax.ShapeDtypeStruct((b, lp, SB_WIDTH), BF16),
        scratch_shapes=[
            pltpu.VMEM((tq, pair), F32),
            pltpu.VMEM((tq, 2 * tk), F32),
        ],
        compiler_params=pltpu.CompilerParams(
            dimension_semantics=("arbitrary", "arbitrary", "arbitrary"),
            vmem_limit_bytes=VMEM_LIMIT_BYTES),
        name="stick_breaking",
    )(proj3, proj3, proj3, _sb_tail_matrix(tk))


def _gla_constants(c, levels, mxu_levels):
    t = np.arange(c)[:, None]
    j = np.arange(c)[None, :]
    blocks = [(j <= t)]
    for l in range(mxu_levels):
        half = 1 << l
        p = (t >> (l + 1) << (l + 1)) + half
        upper = ((t >> l) & 1) == 1
        blocks.append(np.where(upper, (j >= p) & (j <= t), (j > t) & (j < p)))
    e = np.concatenate(blocks, axis=0).astype(np.float32)
    e2 = np.concatenate([e, e], axis=1)
    s = np.arange(c)[None, :]
    x = t ^ s
    lvl = np.where(x > 0, np.floor(np.log2(np.maximum(x, 1))).astype(np.int32), levels)
    lvl = np.where(s > t, -1, lvl).astype(np.int32)
    return jnp.asarray(e2, dtype=BF16), jnp.asarray(lvl)


def _gla_kernel(q_ref, k_ref, v01_ref, v23_ref, r01_ref, r23_ref, lr_ref, wup_ref, bgk_ref,
                gn_ref, e2_ref, lvl_ref, o_ref, st_ref, *, c, n_chunks, levels, mxu_levels):
    dk, dv = GLA_HEAD_DK, GLA_HEAD_DV
    heads = [slice(h * dk, (h + 1) * dk) for h in range(GLA_HEADS)]
    v_refs = (v01_ref, v01_ref, v23_ref, v23_ref)

    @pl.when(pl.program_id(1) == 0)
    def _():
        st_ref[...] = jnp.zeros_like(st_ref)

    rank_lanes = lax.broadcasted_iota(jnp.int32, (c, LANES), 1) < GATE_RANK
    pos = lax.broadcasted_iota(jnp.int32, (c, GLA_K_WIDTH), 0)
    row_bit = [((pos >> l) & 1) == 1 for l in range(mxu_levels)]

    def chunk(rows):
        lr = jnp.where(rank_lanes, lr_ref[rows, :], jnp.zeros((c, LANES), BF16))
        pre = (_dot(lr, wup_ref[...]) + bgk_ref[...]) * LOG2_E
        g = (jnp.minimum(pre, 0.0) - jnp.log2(1.0 + jnp.exp2(-jnp.abs(pre)))) * (1.0 / GATE_TAU)
        g_hi, g_lo = _split_hi_lo(g)
        x = _dot(e2_ref[...], jnp.concatenate([g_hi, g_lo], axis=0))
        cum = x[:c]

        def level_exponent(l):
            if l < mxu_levels:
                return x[(l + 1) * c:(l + 2) * c]
            half = 1 << l
            parts = []
            for base in range(0, c, 2 * half):
                p = base + half
                ref = jnp.broadcast_to(cum[p - 1:p, :], (half, cum.shape[1]))
                parts.append(ref - cum[base:p])
                parts.append(cum[p:p + half] - ref)
            return jnp.concatenate(parts, axis=0)

        q = q_ref[rows, :].astype(F32) * (dk ** -0.5)
        k = k_ref[rows, :].astype(F32)
        lvl = lvl_ref[...]

        def query_or_key(l):
            half = 1 << l
            if half >= SUBLANES:
                return jnp.concatenate(
                    [(q if (r0 // half) % 2 else k)[r0:r0 + half] for r0 in range(0, c, half)],
                    axis=0)
            return jnp.where(row_bit[l], q, k)

        q16, k16 = q.astype(BF16), k.astype(BF16)
        att = [jnp.where(lvl == levels, _dot_nt(q16[:, hs], k16[:, hs]), 0.0) for hs in heads]
        for l in range(levels):
            qk = (query_or_key(l) * jnp.exp2(level_exponent(l))).astype(BF16)
            att = [jnp.where(lvl == l, _dot_nt(qk[:, hs], qk[:, hs]), a)
                   for hs, a in zip(heads, att)]

        last = cum[c - 1:c, :]
        q_dec = (q * jnp.exp2(cum)).astype(BF16)
        k_dec = (k * jnp.exp2(last - cum)).astype(BF16)
        st_decay = jnp.exp2(last)
        outs = []
        for h, hs in enumerate(heads):
            v = v_refs[h][rows, (h % 2) * dv:(h % 2 + 1) * dv]
            st = st_ref[h]
            o = _dot(att[h].astype(BF16), v) + _dot_nt(q_dec[:, hs], st.astype(BF16))
            st_ref[h] = st * st_decay[:, hs] + lax.dot_general(
                v, k_dec[:, hs], (((0,), (0,)), ((), ())), preferred_element_type=F32)
            outs.append(_rms(o, gn_ref[...]))
        r = jnp.concatenate([r01_ref[rows, :], r23_ref[rows, :]], axis=1).astype(F32)
        o_ref[rows, :] = (jnp.concatenate(outs, axis=1) * (r * jax.nn.sigmoid(r))).astype(o_ref.dtype)

    for n in range(n_chunks):
        chunk(slice(n * c, (n + 1) * c))


def _gla(proj3, w_up_pad, b_gk, head_gain, tm):
    b, lp, _ = proj3.shape
    c, levels, mxu_levels = GLA_CHUNK, GLA_LEVELS, GLA_MXU_LEVELS
    e2, lvl = _gla_constants(c, levels, mxu_levels)
    kw, vw, vb = GLA_K_WIDTH, GLA_V_WIDTH, GLA_VBLOCK
    n_chunks = tm // c
    kern = functools.partial(_gla_kernel, c=c, n_chunks=n_chunks, levels=levels,
                             mxu_levels=mxu_levels)
    return pl.pallas_call(
        kern,
        grid=(b, lp // tm),
        in_specs=[
            pl.BlockSpec((None, tm, kw), lambda bi, t: (bi, t, COL_GQ // kw)),
            pl.BlockSpec((None, tm, kw), lambda bi, t: (bi, t, COL_GK // kw)),
            pl.BlockSpec((None, tm, vb), lambda bi, t: (bi, t, COL_GV // vb)),
            pl.BlockSpec((None, tm, vb), lambda bi, t: (bi, t, COL_GV // vb + 1)),
            pl.BlockSpec((None, tm, vb), lambda bi, t: (bi, t, COL_GR // vb)),
            pl.BlockSpec((None, tm, vb), lambda bi, t: (bi, t, COL_GR // vb + 1)),
            pl.BlockSpec((None, tm, LANES), lambda bi, t: (bi, t, COL_LR // LANES)),
            _resident((LANES, kw)),
            _resident((1, kw)),
            _resident((1, GLA_HEAD_DV)),
            _resident(((mxu_levels + 1) * c, 2 * c)),
            _resident((c, c)),
        ],
        out_specs=pl.BlockSpec((None, tm, vw), lambda bi, t: (bi, t, 0)),
        out_shape=jax.ShapeDtypeStruct((b, lp, vw), BF16),
        scratch_shapes=[pltpu.VMEM((GLA_HEADS, GLA_HEAD_DV, GLA_HEAD_DK), F32)],
        compiler_params=pltpu.CompilerParams(
            dimension_semantics=("arbitrary", "arbitrary"),
            vmem_limit_bytes=VMEM_LIMIT_BYTES),
        name="gla",
    )(proj3, proj3, proj3, proj3, proj3, proj3, proj3, w_up_pad, b_gk, head_gain, e2, lvl)


def _merge_kernel(x_ref, meta_ref, gpre_ref, wm_ref, sb_ref, og_ref, wsb_ref, wgla_ref, wo_ref,
                  gpost_ref, o_ref, h_ref, wm16_ref, *, tm):
    @pl.when(jnp.logical_and(pl.program_id(0) == 0, pl.program_id(1) == 0))
    def _():
        for c0 in range(0, 2 * D_MODEL, MXU_WIDTH):
            cols = slice(c0, c0 + MXU_WIDTH)
            wm16_ref[:, cols] = wm_ref[cols, :].T.astype(BF16)

    h = _padded_rows(x_ref, meta_ref, h_ref, tm)
    hn = _rms(h, gpre_ref[...]).astype(BF16)
    m = _dot(hn, wm16_ref[...])
    a = _dot(sb_ref[...], wsb_ref[...])
    b = _dot(og_ref[...], wgla_ref[...])
    mixed = jax.nn.sigmoid(m[:, :D_MODEL]) * a + jax.nn.sigmoid(m[:, D_MODEL:]) * b
    mix = _dot(mixed.astype(BF16), wo_ref[...])
    o_ref[...] = h + _rms(mix, gpost_ref[...])


def _merge(x, meta, g_pre, w_in_t, sb, og, w_sb, w_gla, w_o, g_post, tm, lp):
    b = x.shape[0]
    kern = functools.partial(_merge_kernel, tm=tm)
    merge_rows = pl.BlockSpec(
        (pl.Squeezed(), pl.Element(2 * D_MODEL), pl.Element(D_MODEL)),
        lambda bi, i: (0, COL_MERGE, 0), pipeline_mode=pl.Buffered(1))
    return pl.pallas_call(
        kern,
        grid=(b, lp // tm),
        in_specs=[
            _x_window(tm),
            _resident((N_META, D_MODEL)),
            _resident((1, D_MODEL)),
            merge_rows,
            pl.BlockSpec((None, tm, SB_WIDTH), lambda bi, i: (bi, i, 0)),
            pl.BlockSpec((None, tm, GLA_V_WIDTH), lambda bi, i: (bi, i, 0)),
            _resident((SB_WIDTH, D_MODEL)),
            _resident((GLA_V_WIDTH, D_MODEL)),
            _resident((D_MODEL, D_MODEL)),
            _resident((1, D_MODEL)),
        ],
        out_specs=pl.BlockSpec((None, tm, D_MODEL), lambda bi, i: (bi, i, 0)),
        out_shape=jax.ShapeDtypeStruct((b, lp, D_MODEL), F32),
        scratch_shapes=[
            pltpu.VMEM((tm, D_MODEL), F32),
            pltpu.VMEM((D_MODEL, 2 * D_MODEL), BF16),
        ],
        compiler_params=pltpu.CompilerParams(
            dimension_semantics=("arbitrary", "arbitrary"), vmem_limit_bytes=VMEM_LIMIT_BYTES),
        name="merge",
    )(x, meta, g_pre, w_in_t, sb, og, w_sb, w_gla, w_o, g_post)


def _ffn_kernel(h_ref, hist_ref, gpre_ref, wup_ref, wgate_ref, cw_ref, cb_ref, wdown_ref,
                gpost_ref, o_ref, tail_ref, act_ref, *, tm):
    chunks = [slice(c0, c0 + FF_CHUNK) for c0 in range(0, D_FF, FF_CHUNK)]

    @pl.when(pl.program_id(1) == 0)
    def _():
        hist = _rms(hist_ref[...], gpre_ref[...]).astype(BF16)
        for cols in chunks:
            tail_ref[:, cols] = _dot(hist, wup_ref[:, cols])

    h = h_ref[...]
    hn = _rms(h, gpre_ref[...]).astype(BF16)
    row = lax.broadcasted_iota(jnp.int32, (SUBLANES, FF_CHUNK), 0)
    for cols in chunks:
        up = _dot(hn, wup_ref[:, cols])
        gate = _dot(hn, wgate_ref[:, cols])
        prev = tail_ref[:, cols]
        r1 = pltpu.roll(up, 1, 0)
        r2 = pltpu.roll(up, 2, 0)
        head1 = jnp.where(row == 0, prev[7:8, :], r1[:SUBLANES])
        head2 = jnp.where(row == 0, prev[6:7, :], jnp.where(row == 1, prev[7:8, :], r2[:SUBLANES]))
        up1 = jnp.concatenate([head1, r1[SUBLANES:]], axis=0)
        up2 = jnp.concatenate([head2, r2[SUBLANES:]], axis=0)
        tail_ref[:, cols] = up[tm - SUBLANES:, :]
        cw = cw_ref[:, cols]
        y = cw[0:1, :] * up2 + cw[1:2, :] * up1 + cw[2:3, :] * up + cb_ref[:, cols]
        act_ref[:, cols] = (jax.nn.gelu(y, approximate=True) * gate).astype(BF16)
    ffn = _dot(act_ref[...], wdown_ref[...])
    o_ref[...] = h + _rms(ffn, gpost_ref[...])


def _ffn(h3, g_pre, w_up, w_gate, conv_w, conv_b, w_down, g_post, tm):
    b, lp, _ = h3.shape
    s = lp - FRONT
    kern = functools.partial(_ffn_kernel, tm=tm)

    def rows(n, start):
        return pl.BlockSpec((pl.Squeezed(), pl.Element(n), pl.Element(D_MODEL)), start)

    return pl.pallas_call(
        kern,
        grid=(b, s // tm),
        in_specs=[
            rows(tm, lambda bi, i: (bi, pl.multiple_of(FRONT + i * tm, Q_BLOCK), 0)),
            rows(SUBLANES, lambda bi, i: (bi, FRONT - SUBLANES, 0)),
            _resident((1, D_MODEL)),
            _resident((D_MODEL, D_FF)),
            _resident((D_MODEL, D_FF)),
            _resident((CONV_W, D_FF)),
            _resident((1, D_FF)),
            _resident((D_FF, D_MODEL)),
            _resident((1, D_MODEL)),
        ],
        out_specs=pl.BlockSpec((None, tm, D_MODEL), lambda bi, i: (bi, i, 0)),
        out_shape=jax.ShapeDtypeStruct((b, s, D_MODEL), F32),
        scratch_shapes=[
            pltpu.VMEM((SUBLANES, D_FF), F32),
            pltpu.VMEM((tm, D_FF), BF16),
        ],
        compiler_params=pltpu.CompilerParams(
            dimension_semantics=("arbitrary", "arbitrary"), vmem_limit_bytes=VMEM_LIMIT_BYTES),
        name="ffn",
    )(h3, h3, g_pre, w_up, w_gate, conv_w, conv_b, w_down, g_post)


def _pick_tile(n, want):
    t = want
    while n % t:
        t -= Q_BLOCK
    return t


def kernel(x, meta_tokens, norm_mix_pre, w_in, w_gk_up, b_gk, gla_head_norm, w_sb_out, w_gla_out, w_o,
           norm_mix_post, norm_ffn_pre, w_ffn_up, w_ffn_gate, conv_w, conv_b, w_ffn_down, norm_ffn_post):
    b, s, _ = x.shape
    assert s % Q_BLOCK == 0 and norm_mix_pre.shape[0] == 1
    lp = FRONT + s
    tm = _pick_tile(lp, ROW_TILE)
    tq = _pick_tile(lp, SB_TQ)
    tf = _pick_tile(s, FFN_TILE)
    d = 0

    w_in_t = jnp.swapaxes(w_in, 1, 2)
    w_up_pad = jnp.pad(w_gk_up[d].astype(BF16), ((0, LANES - GATE_RANK), (0, 0)))

    proj3 = _inproj(x, meta_tokens, norm_mix_pre[d][None, :], w_in_t, tm, lp)
    sb = _stick_breaking(proj3, tq, SB_TK)
    og = _gla(proj3, w_up_pad, b_gk[d][None, :], gla_head_norm[d][None, :], tm)
    h1 = _merge(x, meta_tokens, norm_mix_pre[d][None, :], w_in_t, sb, og,
                w_sb_out[d].astype(BF16), w_gla_out[d].astype(BF16), w_o[d].astype(BF16),
                norm_mix_post[d][None, :], tm, lp)
    return _ffn(h1, norm_ffn_pre[d][None, :], w_ffn_up[d].astype(BF16), w_ffn_gate[d].astype(BF16),
                conv_w[d], conv_b[d][None, :], w_ffn_down[d].astype(BF16),
                norm_ffn_post[d][None, :], tf)
```

```python
import functools

import numpy as np
import jax
import jax.numpy as jnp
from jax import lax
from jax.experimental import pallas as pl
from jax.experimental.pallas import tpu as pltpu

F32 = jnp.float32
BF16 = jnp.bfloat16

D_MODEL = 1024
N_META = 16
Q_BLOCK = 128
SB_HEADS = 8
SB_HEAD_DIM = 64
GLA_HEADS = 4
GLA_HEAD_DK = 128
GLA_HEAD_DV = 256
GATE_RANK = 16
GATE_TAU = 16.0
D_FF = 2816
CONV_W = 3
EPS = 1e-6

SB_WIDTH = SB_HEADS * SB_HEAD_DIM
GLA_K_WIDTH = GLA_HEADS * GLA_HEAD_DK
GLA_V_WIDTH = GLA_HEADS * GLA_HEAD_DV

LANES = 128
SUBLANES = 8
MXU_WIDTH = 256
VMEM_LIMIT_BYTES = 56 * 1024 * 1024

COL_SBQ = 0
COL_SBK = COL_SBQ + SB_WIDTH
COL_SBV = COL_SBK + SB_WIDTH
COL_GQ = COL_SBV + SB_WIDTH
COL_GK = COL_GQ + GLA_K_WIDTH
COL_GV = COL_GK + GLA_K_WIDTH
COL_GR = COL_GV + GLA_V_WIDTH
COL_LR = COL_GR + GLA_V_WIDTH
COL_MERGE = COL_LR + GATE_RANK
PROJ_WIDTH = COL_LR + LANES

FRONT = Q_BLOCK
ROW_TILE = 640
FFN_TILE = 1024
SB_TQ = 640
SB_TK = 128
GLA_CHUNK = 128
GLA_LEVELS = GLA_CHUNK.bit_length() - 1
GLA_MXU_LEVELS = SUBLANES.bit_length() - 1
GLA_VBLOCK = 512
FF_CHUNK = 256
LOG2_E = 1.4426950408889634
SB_Q_SCALE = SB_HEAD_DIM ** -0.5 * LOG2_E
GLA_Q_SCALE = GLA_HEAD_DK ** -0.5
SB_LOG2_UNDERFLOW = -126.0
SB_MASKED = -1e30


def _dot(a, b):
    return jnp.dot(a, b, preferred_element_type=F32)


def _dot_nt(a, b):
    return lax.dot_general(a, b, (((1,), (1,)), ((), ())), preferred_element_type=F32)


def _rms(x, gain):
    ms = jnp.mean(x * x, axis=-1, keepdims=True)
    return x * lax.rsqrt(ms + EPS) * gain


def _split_hi_lo(x):
    hi = x.astype(BF16)
    lo = (x - hi.astype(F32)).astype(BF16)
    return hi, lo


def _resident(shape):
    nd = len(shape)
    return pl.BlockSpec(shape, lambda *_: (0,) * nd, pipeline_mode=pl.Buffered(1))


def _x_window(tm):
    return pl.BlockSpec(
        (pl.Squeezed(), pl.Element(tm), pl.Element(D_MODEL)),
        lambda b, i: (b, pl.multiple_of(jnp.maximum(i * tm - FRONT, 0), Q_BLOCK), 0))


def _padded_rows(x_ref, meta_ref, h_ref, tm):
    i = pl.program_id(1)

    @pl.when(i == 0)
    def _():
        h_ref[0:FRONT - N_META, :] = jnp.zeros((FRONT - N_META, D_MODEL), F32)
        h_ref[FRONT - N_META:FRONT, :] = meta_ref[...]
        h_ref[FRONT:, :] = x_ref[0:tm - FRONT, :]

    @pl.when(i > 0)
    def _():
        h_ref[...] = x_ref[...]

    return h_ref[...]


def _inproj_kernel(x_ref, meta_ref, g_ref, w_ref, o_ref, h_ref, w16_ref, *, tm):
    chunks = [slice(c0, min(c0 + MXU_WIDTH, PROJ_WIDTH)) for c0 in range(0, PROJ_WIDTH, MXU_WIDTH)]

    def column_scale(cols):
        if COL_SBQ <= cols.start < COL_SBQ + SB_WIDTH:
            return SB_Q_SCALE
        if COL_GQ <= cols.start < COL_GQ + GLA_K_WIDTH:
            return GLA_Q_SCALE
        return None

    @pl.when(jnp.logical_and(pl.program_id(0) == 0, pl.program_id(1) == 0))
    def _():
        for cols in chunks:
            w = w_ref[cols, :].T
            scale = column_scale(cols)
            w16_ref[:, cols] = (w if scale is None else w * scale).astype(BF16)

    h = _padded_rows(x_ref, meta_ref, h_ref, tm)
    hn = _rms(h, g_ref[...]).astype(BF16)
    for cols in chunks:
        o_ref[:, cols] = _dot(hn, w16_ref[:, cols]).astype(BF16)


def _inproj(x, meta, gain, w_in_t, tm, lp):
    b = x.shape[0]
    kern = functools.partial(_inproj_kernel, tm=tm)
    return pl.pallas_call(
        kern,
        grid=(b, lp // tm),
        in_specs=[
            _x_window(tm),
            _resident((N_META, D_MODEL)),
            _resident((1, D_MODEL)),
            pl.BlockSpec((None, PROJ_WIDTH, D_MODEL), lambda bi, i: (0, 0, 0),
                         pipeline_mode=pl.Buffered(1)),
        ],
        out_specs=pl.BlockSpec((None, tm, PROJ_WIDTH), lambda bi, i: (bi, i, 0)),
        out_shape=jax.ShapeDtypeStruct((b, lp, PROJ_WIDTH), BF16),
        scratch_shapes=[
            pltpu.VMEM((tm, D_MODEL), F32),
            pltpu.VMEM((D_MODEL, PROJ_WIDTH), BF16),
        ],
        compiler_params=pltpu.CompilerParams(
            dimension_semantics=("arbitrary", "arbitrary"), vmem_limit_bytes=VMEM_LIMIT_BYTES),
        name="inproj",
    )(x, meta, gain, w_in_t)


def _sb_tail_matrix(tk):
    j = np.arange(tk)
    later = (j[:, None] > j[None, :]).astype(np.float32)
    ones, zero = np.ones((tk, tk), np.float32), np.zeros((tk, tk), np.float32)
    return jnp.asarray(-np.block([[later, zero, ones, zero], [zero, later, zero, ones]]), dtype=BF16)


def _sb_kernel(q_ref, k_ref, v_ref, tt_ref, o_ref, acc_ref, c_ref, *, tq, tk):
    i = pl.program_id(2)
    nd = tq // tk
    q = q_ref[...]
    tt = tt_ref[...]
    head_a = lax.broadcasted_iota(jnp.int32, (tk, 2 * SB_HEAD_DIM), 1) < SB_HEAD_DIM
    col = lax.broadcasted_iota(jnp.int32, (tk, 2 * tk), 1)
    key = jnp.where(col >= tk, col - tk, col)
    row = lax.broadcasted_iota(jnp.int32, (tk, 2 * tk), 0)
    causal_bias = jnp.where(key < row, 0.0, SB_MASKED)

    def block_diag(x):
        zero = jnp.zeros_like(x)
        return jnp.concatenate([jnp.where(head_a, x, zero), jnp.where(head_a, zero, x)], axis=0)

    def step(delta, diagonal):
        zs, v_bds = [], []
        for r in range(nd):
            jb = i * nd + r - delta
            start = pl.multiple_of(jnp.maximum(jb, 0) * tk, tk)
            k_bd = block_diag(k_ref[pl.ds(start, tk), :])
            v_bds.append(block_diag(v_ref[pl.ds(start, tk), :]))
            z = _dot_nt(q[r * tk:(r + 1) * tk, :], k_bd)
            first_key = jnp.where(jb == 0, FRONT - N_META, jnp.where(jb < 0, tk, 0))
            z = z + jnp.where(key[0:1, :] >= first_key, 0.0, SB_MASKED)
            if diagonal:
                z = z + causal_bias
            zs.append(z)
        z = jnp.concatenate(zs, axis=0)
        s = jnp.maximum(z, 0.0) + jnp.log2(1.0 + jnp.exp2(-jnp.abs(z)))
        log_beta = z - s
        sums = _dot(s.astype(BF16), tt)
        carry = c_ref[...]
        tail = sums[:, :2 * tk] + carry
        w = jnp.exp2(log_beta + tail).astype(BF16)
        carry = carry + sums[:, 2 * tk:]
        c_ref[...] = carry
        acc_ref[...] += jnp.concatenate(
            [_dot(w[r * tk:(r + 1) * tk, :], v_bds[r]) for r in range(nd)], axis=0)
        return jnp.max(jnp.max(carry, axis=0, keepdims=True), axis=1, keepdims=True)[0, 0]

    acc_ref[...] = jnp.zeros_like(acc_ref)
    c_ref[...] = jnp.zeros_like(c_ref)
    step(0, True)
    stick = step(1, False)

    def live(state):
        delta, stick = state
        return jnp.logical_and(delta < (i + 1) * nd, stick > SB_LOG2_UNDERFLOW)

    lax.while_loop(live, lambda state: (state[0] + 1, step(state[0], False)), (2, stick))
    o_ref[...] = acc_ref[...].astype(o_ref.dtype)


def _stick_breaking(proj3, tq, tk):
    b, lp, _ = proj3.shape
    pair = 2 * SB_HEAD_DIM
    n_pairs = SB_WIDTH // pair
    kern = functools.partial(_sb_kernel, tq=tq, tk=tk)
    return pl.pallas_call(
        kern,
        grid=(b, n_pairs, lp // tq),
        in_specs=[
            pl.BlockSpec((None, tq, pair), lambda bi, hp, i: (bi, i, COL_SBQ // pair + hp)),
            pl.BlockSpec((None, lp, pair), lambda bi, hp, i: (bi, 0, COL_SBK // pair + hp)),
            pl.BlockSpec((None, lp, pair), lambda bi, hp, i: (bi, 0, COL_SBV // pair + hp)),
            _resident((2 * tk, 4 * tk)),
        ],
        out_specs=pl.BlockSpec((None, tq, pair), lambda bi, hp, i: (bi, i, hp)),
        out_shape=jax.ShapeDtypeStruct((b, lp, SB_WIDTH), BF16),
        scratch_shapes=[
            pltpu.VMEM((tq, pair), F32),
            pltpu.VMEM((tq, 2 * tk), F32),
        ],
        compiler_params=pltpu.CompilerParams(
            dimension_semantics=("arbitrary", "arbitrary", "arbitrary"),
            vmem_limit_bytes=VMEM_LIMIT_BYTES),
        name="stick_breaking",
    )(proj3, proj3, proj3, _sb_tail_matrix(tk))


def _gla_constants(c, levels, mxu_levels):
    t = np.arange(c)[:, None]
    j = np.arange(c)[None, :]
    blocks = [(j <= t)]
    for l in range(mxu_levels):
        half = 1 << l
        p = (t >> (l + 1) << (l + 1)) + half
        upper = ((t >> l) & 1) == 1
        blocks.append(np.where(upper, (j >= p) & (j <= t), (j > t) & (j < p)))
    e = np.concatenate(blocks, axis=0).astype(np.float32)
    e2 = np.concatenate([e, e], axis=1)
    s = np.arange(c)[None, :]
    x = t ^ s
    lvl = np.where(x > 0, np.floor(np.log2(np.maximum(x, 1))).astype(np.int32), levels)
    lvl = np.where(s > t, -1, lvl).astype(np.int32)
    return jnp.asarray(e2, dtype=BF16), jnp.asarray(lvl)


def _gla_kernel(q_ref, k_ref, v01_ref, v23_ref, r01_ref, r23_ref, lr_ref, wup_ref, bgk_ref,
                gn_ref, e2_ref, lvl_ref, o_ref, st_ref, *, c, n_chunks, levels, mxu_levels):
    dk, dv = GLA_HEAD_DK, GLA_HEAD_DV
    heads = [slice(h * dk, (h + 1) * dk) for h in range(GLA_HEADS)]
    v_refs = (v01_ref, v01_ref, v23_ref, v23_ref)

    @pl.when(pl.program_id(1) == 0)
    def _():
        st_ref[...] = jnp.zeros_like(st_ref)

    rank_lanes = lax.broadcasted_iota(jnp.int32, (c, LANES), 1) < GATE_RANK
    pos = lax.broadcasted_iota(jnp.int32, (c, GLA_K_WIDTH), 0)
    row_bit = [((pos >> l) & 1) == 1 for l in range(mxu_levels)]

    def chunk(rows):
        lr = jnp.where(rank_lanes, lr_ref[rows, :], jnp.zeros((c, LANES), BF16))
        pre = _dot(lr, wup_ref[...]) + bgk_ref[...]
        g = (jnp.minimum(pre, 0.0) - jnp.log2(1.0 + jnp.exp2(-jnp.abs(pre)))) * (1.0 / GATE_TAU)
        g_hi, g_lo = _split_hi_lo(g)
        x = _dot(e2_ref[...], jnp.concatenate([g_hi, g_lo], axis=0))
        cum = x[:c]

        def level_exponent(l):
            if l < mxu_levels:
                return x[(l + 1) * c:(l + 2) * c]
            half = 1 << l
            parts = []
            for base in range(0, c, 2 * half):
                p = base + half
                ref = jnp.broadcast_to(cum[p - 1:p, :], (half, cum.shape[1]))
                parts.append(ref - cum[base:p])
                parts.append(cum[p:p + half] - ref)
            return jnp.concatenate(parts, axis=0)

        q = q_ref[rows, :].astype(F32)
        k = k_ref[rows, :].astype(F32)
        lvl = lvl_ref[...]

        def query_or_key(l):
            half = 1 << l
            if half >= SUBLANES:
                return jnp.concatenate(
                    [(q if (r0 // half) % 2 else k)[r0:r0 + half] for r0 in range(0, c, half)],
                    axis=0)
            return jnp.where(row_bit[l], q, k)

        q16, k16 = q.astype(BF16), k.astype(BF16)
        att = [jnp.where(lvl == levels, _dot_nt(q16[:, hs], k16[:, hs]), 0.0) for hs in heads]
        for l in range(levels):
            qk = (query_or_key(l) * jnp.exp2(level_exponent(l))).astype(BF16)
            att = [jnp.where(lvl == l, _dot_nt(qk[:, hs], qk[:, hs]), a)
                   for hs, a in zip(heads, att)]

        last = cum[c - 1:c, :]
        q_dec = (q * jnp.exp2(cum)).astype(BF16)
        k_dec = (k * jnp.exp2(last - cum)).astype(BF16)
        st_decay = jnp.exp2(last)
        outs = []
        for h, hs in enumerate(heads):
            v = v_refs[h][rows, (h % 2) * dv:(h % 2 + 1) * dv]
            st = st_ref[h]
            o = _dot(att[h].astype(BF16), v) + _dot_nt(q_dec[:, hs], st.astype(BF16))
            st_ref[h] = st * st_decay[:, hs] + lax.dot_general(
                v, k_dec[:, hs], (((0,), (0,)), ((), ())), preferred_element_type=F32)
            outs.append(_rms(o, gn_ref[...]))
        r = jnp.concatenate([r01_ref[rows, :], r23_ref[rows, :]], axis=1).astype(F32)
        o_ref[rows, :] = (jnp.concatenate(outs, axis=1) * (r * jax.nn.sigmoid(r))).astype(o_ref.dtype)

    for n in range(n_chunks):
        chunk(slice(n * c, (n + 1) * c))


def _gla(proj3, w_up_pad, b_gk, head_gain, tm):
    b, lp, _ = proj3.shape
    c, levels, mxu_levels = GLA_CHUNK, GLA_LEVELS, GLA_MXU_LEVELS
    e2, lvl = _gla_constants(c, levels, mxu_levels)
    kw, vw, vb = GLA_K_WIDTH, GLA_V_WIDTH, GLA_VBLOCK
    n_chunks = tm // c
    kern = functools.partial(_gla_kernel, c=c, n_chunks=n_chunks, levels=levels,
                             mxu_levels=mxu_levels)
    return pl.pallas_call(
        kern,
        grid=(b, lp // tm),
        in_specs=[
            pl.BlockSpec((None, tm, kw), lambda bi, t: (bi, t, COL_GQ // kw)),
            pl.BlockSpec((None, tm, kw), lambda bi, t: (bi, t, COL_GK // kw)),
            pl.BlockSpec((None, tm, vb), lambda bi, t: (bi, t, COL_GV // vb)),
            pl.BlockSpec((None, tm, vb), lambda bi, t: (bi, t, COL_GV // vb + 1)),
            pl.BlockSpec((None, tm, vb), lambda bi, t: (bi, t, COL_GR // vb)),
            pl.BlockSpec((None, tm, vb), lambda bi, t: (bi, t, COL_GR // vb + 1)),
            pl.BlockSpec((None, tm, LANES), lambda bi, t: (bi, t, COL_LR // LANES)),
            _resident((LANES, kw)),
            _resident((1, kw)),
            _resident((1, GLA_HEAD_DV)),
            _resident(((mxu_levels + 1) * c, 2 * c)),
            _resident((c, c)),
        ],
        out_specs=pl.BlockSpec((None, tm, vw), lambda bi, t: (bi, t, 0)),
        out_shape=jax.ShapeDtypeStruct((b, lp, vw), BF16),
        scratch_shapes=[pltpu.VMEM((GLA_HEADS, GLA_HEAD_DV, GLA_HEAD_DK), F32)],
        compiler_params=pltpu.CompilerParams(
            dimension_semantics=("arbitrary", "arbitrary"),
            vmem_limit_bytes=VMEM_LIMIT_BYTES),
        name="gla",
    )(proj3, proj3, proj3, proj3, proj3, proj3, proj3, w_up_pad, b_gk, head_gain, e2, lvl)


def _merge_kernel(x_ref, meta_ref, gpre_ref, wm_ref, sb_ref, og_ref, wsb_ref, wgla_ref, wo_ref,
                  gpost_ref, o_ref, h_ref, wm16_ref, *, tm):
    @pl.when(jnp.logical_and(pl.program_id(0) == 0, pl.program_id(1) == 0))
    def _():
        for c0 in range(0, 2 * D_MODEL, MXU_WIDTH):
            cols = slice(c0, c0 + MXU_WIDTH)
            wm16_ref[:, cols] = wm_ref[cols, :].T.astype(BF16)

    h = _padded_rows(x_ref, meta_ref, h_ref, tm)
    hn = _rms(h, gpre_ref[...]).astype(BF16)
    m = _dot(hn, wm16_ref[...])
    a = _dot(sb_ref[...], wsb_ref[...])
    b = _dot(og_ref[...], wgla_ref[...])
    mixed = jax.nn.sigmoid(m[:, :D_MODEL]) * a + jax.nn.sigmoid(m[:, D_MODEL:]) * b
    mix = _dot(mixed.astype(BF16), wo_ref[...])
    o_ref[...] = h + _rms(mix, gpost_ref[...])


def _merge(x, meta, g_pre, w_in_t, sb, og, w_sb, w_gla, w_o, g_post, tm, lp):
    b = x.shape[0]
    kern = functools.partial(_merge_kernel, tm=tm)
    merge_rows = pl.BlockSpec(
        (pl.Squeezed(), pl.Element(2 * D_MODEL), pl.Element(D_MODEL)),
        lambda bi, i: (0, COL_MERGE, 0), pipeline_mode=pl.Buffered(1))
    return pl.pallas_call(
        kern,
        grid=(b, lp // tm),
        in_specs=[
            _x_window(tm),
            _resident((N_META, D_MODEL)),
            _resident((1, D_MODEL)),
            merge_rows,
            pl.BlockSpec((None, tm, SB_WIDTH), lambda bi, i: (bi, i, 0)),
            pl.BlockSpec((None, tm, GLA_V_WIDTH), lambda bi, i: (bi, i, 0)),
            _resident((SB_WIDTH, D_MODEL)),
            _resident((GLA_V_WIDTH, D_MODEL)),
            _resident((D_MODEL, D_MODEL)),
            _resident((1, D_MODEL)),
        ],
        out_specs=pl.BlockSpec((None, tm, D_MODEL), lambda bi, i: (bi, i, 0)),
        out_shape=jax.ShapeDtypeStruct((b, lp, D_MODEL), F32),
        scratch_shapes=[
            pltpu.VMEM((tm, D_MODEL), F32),
            pltpu.VMEM((D_MODEL, 2 * D_MODEL), BF16),
        ],
        compiler_params=pltpu.CompilerParams(
            dimension_semantics=("arbitrary", "arbitrary"), vmem_limit_bytes=VMEM_LIMIT_BYTES),
        name="merge",
    )(x, meta, g_pre, w_in_t, sb, og, w_sb, w_gla, w_o, g_post)


def _ffn_kernel(h_ref, hist_ref, gpre_ref, wup_ref, wgate_ref, cw_ref, cb_ref, wdown_ref,
                gpost_ref, o_ref, tail_ref, act_ref, *, tm):
    chunks = [slice(c0, c0 + FF_CHUNK) for c0 in range(0, D_FF, FF_CHUNK)]

    @pl.when(pl.program_id(1) == 0)
    def _():
        hist = _rms(hist_ref[...], gpre_ref[...]).astype(BF16)
        for cols in chunks:
            tail_ref[:, cols] = _dot(hist, wup_ref[:, cols])

    h = h_ref[...]
    hn = _rms(h, gpre_ref[...]).astype(BF16)
    row = lax.broadcasted_iota(jnp.int32, (SUBLANES, FF_CHUNK), 0)
    for cols in chunks:
        up = _dot(hn, wup_ref[:, cols])
        gate = _dot(hn, wgate_ref[:, cols])
        prev = tail_ref[:, cols]
        r1 = pltpu.roll(up, 1, 0)
        r2 = pltpu.roll(up, 2, 0)
        last, before_last = prev[SUBLANES - 1:SUBLANES, :], prev[SUBLANES - 2:SUBLANES - 1, :]
        head1 = jnp.where(row == 0, last, r1[:SUBLANES])
        head2 = jnp.where(row == 0, before_last, jnp.where(row == 1, last, r2[:SUBLANES]))
        up1 = jnp.concatenate([head1, r1[SUBLANES:]], axis=0)
        up2 = jnp.concatenate([head2, r2[SUBLANES:]], axis=0)
        tail_ref[:, cols] = up[tm - SUBLANES:, :]
        cw = cw_ref[:, cols]
        y = cw[0:1, :] * up2 + cw[1:2, :] * up1 + cw[2:3, :] * up + cb_ref[:, cols]
        act_ref[:, cols] = (jax.nn.gelu(y, approximate=True) * gate).astype(BF16)
    ffn = _dot(act_ref[...], wdown_ref[...])
    o_ref[...] = h + _rms(ffn, gpost_ref[...])


def _ffn(h3, g_pre, w_up, w_gate, conv_w, conv_b, w_down, g_post, tm):
    b, lp, _ = h3.shape
    s = lp - FRONT
    kern = functools.partial(_ffn_kernel, tm=tm)

    def rows(n, start):
        return pl.BlockSpec((pl.Squeezed(), pl.Element(n), pl.Element(D_MODEL)), start)

    return pl.pallas_call(
        kern,
        grid=(b, s // tm),
        in_specs=[
            rows(tm, lambda bi, i: (bi, pl.multiple_of(FRONT + i * tm, Q_BLOCK), 0)),
            rows(SUBLANES, lambda bi, i: (bi, FRONT - SUBLANES, 0)),
            _resident((1, D_MODEL)),
            _resident((D_MODEL, D_FF)),
            _resident((D_MODEL, D_FF)),
            _resident((CONV_W, D_FF)),
            _resident((1, D_FF)),
            _resident((D_FF, D_MODEL)),
            _resident((1, D_MODEL)),
        ],
        out_specs=pl.BlockSpec((None, tm, D_MODEL), lambda bi, i: (bi, i, 0)),
        out_shape=jax.ShapeDtypeStruct((b, s, D_MODEL), F32),
        scratch_shapes=[
            pltpu.VMEM((SUBLANES, D_FF), F32),
            pltpu.VMEM((tm, D_FF), BF16),
        ],
        compiler_params=pltpu.CompilerParams(
            dimension_semantics=("arbitrary", "arbitrary"), vmem_limit_bytes=VMEM_LIMIT_BYTES),
        name="ffn",
    )(h3, h3, g_pre, w_up, w_gate, conv_w, conv_b, w_down, g_post)


def _pick_tile(n, want):
    t = want
    while n % t:
        t -= Q_BLOCK
    return t


def kernel(x, meta_tokens, norm_mix_pre, w_in, w_gk_up, b_gk, gla_head_norm, w_sb_out, w_gla_out, w_o,
           norm_mix_post, norm_ffn_pre, w_ffn_up, w_ffn_gate, conv_w, conv_b, w_ffn_down, norm_ffn_post):
    b, s, _ = x.shape
    assert s % Q_BLOCK == 0 and norm_mix_pre.shape[0] == 1
    lp = FRONT + s
    tm = _pick_tile(lp, ROW_TILE)
    tq = _pick_tile(lp, SB_TQ)
    tf = _pick_tile(s, FFN_TILE)
    d = 0

    w_in_t = jnp.swapaxes(w_in, 1, 2)
    w_up_pad = jnp.pad((w_gk_up[d] * LOG2_E).astype(BF16), ((0, LANES - GATE_RANK), (0, 0)))
    b_gk2 = b_gk[d][None, :] * LOG2_E

    proj3 = _inproj(x, meta_tokens, norm_mix_pre[d][None, :], w_in_t, tm, lp)
    sb = _stick_breaking(proj3, tq, SB_TK)
    og = _gla(proj3, w_up_pad, b_gk2, gla_head_norm[d][None, :], tm)
    h1 = _merge(x, meta_tokens, norm_mix_pre[d][None, :], w_in_t, sb, og,
                w_sb_out[d].astype(BF16), w_gla_out[d].astype(BF16), w_o[d].astype(BF16),
                norm_mix_post[d][None, :], tm, lp)
    return _ffn(h1, norm_ffn_pre[d][None, :], w_ffn_up[d].astype(BF16), w_ffn_gate[d].astype(BF16),
                conv_w[d], conv_b[d][None, :], w_ffn_down[d].astype(BF16),
                norm_ffn_post[d][None, :], tf)
```

```python
import functools

import numpy as np
import jax
import jax.numpy as jnp
from jax import lax
from jax.experimental import pallas as pl
from jax.experimental.pallas import tpu as pltpu

F32 = jnp.float32
BF16 = jnp.bfloat16

D_MODEL = 1024
N_META = 16
Q_BLOCK = 128
SB_HEADS = 8
SB_HEAD_DIM = 64
GLA_HEADS = 4
GLA_HEAD_DK = 128
GLA_HEAD_DV = 256
GATE_RANK = 16
GATE_TAU = 16.0
D_FF = 2816
CONV_W = 3
EPS = 1e-6

SB_WIDTH = SB_HEADS * SB_HEAD_DIM
GLA_K_WIDTH = GLA_HEADS * GLA_HEAD_DK
GLA_V_WIDTH = GLA_HEADS * GLA_HEAD_DV

LANES = 128
SUBLANES = 8
MXU_WIDTH = 256
VMEM_LIMIT_BYTES = 56 * 1024 * 1024

COL_SBQ = 0
COL_SBK = COL_SBQ + SB_WIDTH
COL_SBV = COL_SBK + SB_WIDTH
COL_GQ = COL_SBV + SB_WIDTH
COL_GK = COL_GQ + GLA_K_WIDTH
COL_GV = COL_GK + GLA_K_WIDTH
COL_GR = COL_GV + GLA_V_WIDTH
COL_LR = COL_GR + GLA_V_WIDTH
COL_MERGE = COL_LR + GATE_RANK
PROJ_WIDTH = COL_LR + LANES

FRONT = Q_BLOCK
ROW_TILE = 640
FFN_TILE = 1024
SB_TQ = 640
SB_TK = 128
GLA_CHUNK = 128
GLA_LEVELS = GLA_CHUNK.bit_length() - 1
GLA_MXU_LEVELS = SUBLANES.bit_length() - 1
GLA_VBLOCK = 512
FF_CHUNK = 256
FFN_CAST_BLOCKS = 8
LOG2_E = 1.4426950408889634
SB_Q_SCALE = SB_HEAD_DIM ** -0.5 * LOG2_E
GLA_Q_SCALE = GLA_HEAD_DK ** -0.5
SB_LOG2_UNDERFLOW = -126.0
SB_MASKED = -1e30


def _dot(a, b):
    return jnp.dot(a, b, preferred_element_type=F32)


def _dot_nt(a, b):
    return lax.dot_general(a, b, (((1,), (1,)), ((), ())), preferred_element_type=F32)


def _rms(x, gain):
    ms = jnp.mean(x * x, axis=-1, keepdims=True)
    return x * lax.rsqrt(ms + EPS) * gain


def _split_hi_lo(x):
    hi = x.astype(BF16)
    lo = (x - hi.astype(F32)).astype(BF16)
    return hi, lo


def _resident(shape):
    nd = len(shape)
    return pl.BlockSpec(shape, lambda *_: (0,) * nd, pipeline_mode=pl.Buffered(1))


def _x_window(tm):
    return pl.BlockSpec(
        (pl.Squeezed(), pl.Element(tm), pl.Element(D_MODEL)),
        lambda b, i: (b, pl.multiple_of(jnp.maximum(i * tm - FRONT, 0), Q_BLOCK), 0))


def _padded_rows(x_ref, meta_ref, h_ref, tm):
    i = pl.program_id(1)

    @pl.when(i == 0)
    def _():
        h_ref[0:FRONT - N_META, :] = jnp.zeros((FRONT - N_META, D_MODEL), F32)
        h_ref[FRONT - N_META:FRONT, :] = meta_ref[...]
        h_ref[FRONT:, :] = x_ref[0:tm - FRONT, :]

    @pl.when(i > 0)
    def _():
        h_ref[...] = x_ref[...]

    return h_ref[...]


def _inproj_kernel(x_ref, meta_ref, g_ref, w_ref, o_ref, h_ref, w16_ref, *, tm):
    chunks = [slice(c0, min(c0 + MXU_WIDTH, PROJ_WIDTH)) for c0 in range(0, PROJ_WIDTH, MXU_WIDTH)]

    def column_scale(cols):
        if COL_SBQ <= cols.start < COL_SBQ + SB_WIDTH:
            return SB_Q_SCALE
        if COL_GQ <= cols.start < COL_GQ + GLA_K_WIDTH:
            return GLA_Q_SCALE
        return None

    @pl.when(jnp.logical_and(pl.program_id(0) == 0, pl.program_id(1) == 0))
    def _():
        for cols in chunks:
            w = w_ref[cols, :].T
            scale = column_scale(cols)
            w16_ref[:, cols] = (w if scale is None else w * scale).astype(BF16)

    h = _padded_rows(x_ref, meta_ref, h_ref, tm)
    hn = _rms(h, g_ref[...]).astype(BF16)
    for cols in chunks:
        o_ref[:, cols] = _dot(hn, w16_ref[:, cols]).astype(BF16)


def _inproj(x, meta, gain, w_in_t, tm, lp):
    b = x.shape[0]
    kern = functools.partial(_inproj_kernel, tm=tm)
    return pl.pallas_call(
        kern,
        grid=(b, lp // tm),
        in_specs=[
            _x_window(tm),
            _resident((N_META, D_MODEL)),
            _resident((1, D_MODEL)),
            pl.BlockSpec((None, PROJ_WIDTH, D_MODEL), lambda bi, i: (0, 0, 0),
                         pipeline_mode=pl.Buffered(1)),
        ],
        out_specs=pl.BlockSpec((None, tm, PROJ_WIDTH), lambda bi, i: (bi, i, 0)),
        out_shape=jax.ShapeDtypeStruct((b, lp, PROJ_WIDTH), BF16),
        scratch_shapes=[
            pltpu.VMEM((tm, D_MODEL), F32),
            pltpu.VMEM((D_MODEL, PROJ_WIDTH), BF16),
        ],
        compiler_params=pltpu.CompilerParams(
            dimension_semantics=("arbitrary", "arbitrary"), vmem_limit_bytes=VMEM_LIMIT_BYTES),
        name="inproj",
    )(x, meta, gain, w_in_t)


def _sb_tail_matrix(tk):
    j = np.arange(tk)
    later = (j[:, None] > j[None, :]).astype(np.float32)
    ones, zero = np.ones((tk, tk), np.float32), np.zeros((tk, tk), np.float32)
    return jnp.asarray(-np.block([[later, zero, ones, zero], [zero, later, zero, ones]]), dtype=BF16)


def _sb_kernel(q_ref, k_ref, v_ref, tt_ref, o_ref, acc_ref, c_ref, *, tq, tk):
    i = pl.program_id(2)
    nd = tq // tk
    q = q_ref[...]
    tt = tt_ref[...]
    head_a = lax.broadcasted_iota(jnp.int32, (tk, 2 * SB_HEAD_DIM), 1) < SB_HEAD_DIM
    col = lax.broadcasted_iota(jnp.int32, (tk, 2 * tk), 1)
    key = jnp.where(col >= tk, col - tk, col)
    row = lax.broadcasted_iota(jnp.int32, (tk, 2 * tk), 0)
    causal_bias = jnp.where(key < row, 0.0, SB_MASKED)

    def block_diag(x):
        zero = jnp.zeros_like(x)
        return jnp.concatenate([jnp.where(head_a, x, zero), jnp.where(head_a, zero, x)], axis=0)

    def step(delta, diagonal):
        zs, v_bds = [], []
        for r in range(nd):
            jb = i * nd + r - delta
            start = pl.multiple_of(jnp.maximum(jb, 0) * tk, tk)
            k_bd = block_diag(k_ref[pl.ds(start, tk), :])
            v_bds.append(block_diag(v_ref[pl.ds(start, tk), :]))
            z = _dot_nt(q[r * tk:(r + 1) * tk, :], k_bd)
            first_key = jnp.where(jb == 0, FRONT - N_META, jnp.where(jb < 0, tk, 0))
            z = z + jnp.where(key[0:1, :] >= first_key, 0.0, SB_MASKED)
            if diagonal:
                z = z + causal_bias
            zs.append(z)
        z = jnp.concatenate(zs, axis=0)
        s = jnp.maximum(z, 0.0) + jnp.log2(1.0 + jnp.exp2(-jnp.abs(z)))
        log_beta = z - s
        sums = _dot(s.astype(BF16), tt)
        carry = c_ref[...]
        tail = sums[:, :2 * tk] + carry
        w = jnp.exp2(log_beta + tail).astype(BF16)
        carry = carry + sums[:, 2 * tk:]
        c_ref[...] = carry
        acc_ref[...] += jnp.concatenate(
            [_dot(w[r * tk:(r + 1) * tk, :], v_bds[r]) for r in range(nd)], axis=0)
        return jnp.max(jnp.max(carry, axis=0, keepdims=True), axis=1, keepdims=True)[0, 0]

    acc_ref[...] = jnp.zeros_like(acc_ref)
    c_ref[...] = jnp.zeros_like(c_ref)
    step(0, True)
    stick = step(1, False)

    def live(state):
        delta, stick = state
        return jnp.logical_and(delta < (i + 1) * nd, stick > SB_LOG2_UNDERFLOW)

    lax.while_loop(live, lambda state: (state[0] + 1, step(state[0], False)), (2, stick))
    o_ref[...] = acc_ref[...].astype(o_ref.dtype)


def _stick_breaking(proj3, tq, tk):
    b, lp, _ = proj3.shape
    pair = 2 * SB_HEAD_DIM
    n_pairs = SB_WIDTH // pair
    kern = functools.partial(_sb_kernel, tq=tq, tk=tk)
    return pl.pallas_call(
        kern,
        grid=(b, n_pairs, lp // tq),
        in_specs=[
            pl.BlockSpec((None, tq, pair), lambda bi, hp, i: (bi, i, COL_SBQ // pair + hp)),
            pl.BlockSpec((None, lp, pair), lambda bi, hp, i: (bi, 0, COL_SBK // pair + hp)),
            pl.BlockSpec((None, lp, pair), lambda bi, hp, i: (bi, 0, COL_SBV // pair + hp)),
            _resident((2 * tk, 4 * tk)),
        ],
        out_specs=pl.BlockSpec((None, tq, pair), lambda bi, hp, i: (bi, i, hp)),
        out_shape=jax.ShapeDtypeStruct((b, lp, SB_WIDTH), BF16),
        scratch_shapes=[
            pltpu.VMEM((tq, pair), F32),
            pltpu.VMEM((tq, 2 * tk), F32),
        ],
        compiler_params=pltpu.CompilerParams(
            dimension_semantics=("arbitrary", "arbitrary", "arbitrary"),
            vmem_limit_bytes=VMEM_LIMIT_BYTES),
        name="stick_breaking",
    )(proj3, proj3, proj3, _sb_tail_matrix(tk))


def _gla_constants(c, levels, mxu_levels):
    t = np.arange(c)[:, None]
    j = np.arange(c)[None, :]
    blocks = [(j <= t)]
    for l in range(mxu_levels):
        half = 1 << l
        p = (t >> (l + 1) << (l + 1)) + half
        upper = ((t >> l) & 1) == 1
        blocks.append(np.where(upper, (j >= p) & (j <= t), (j > t) & (j < p)))
    e = np.concatenate(blocks, axis=0).astype(np.float32)
    e2 = np.concatenate([e, e], axis=1)
    s = np.arange(c)[None, :]
    x = t ^ s
    lvl = np.where(x > 0, np.floor(np.log2(np.maximum(x, 1))).astype(np.int32), levels)
    lvl = np.where(s > t, -1, lvl).astype(np.int32)
    return jnp.asarray(e2, dtype=BF16), jnp.asarray(lvl)


def _gla_kernel(q_ref, k_ref, v01_ref, v23_ref, r01_ref, r23_ref, lr_ref, wup_ref, bgk_ref,
                gn_ref, e2_ref, lvl_ref, o_ref, st_ref, *, c, n_chunks, levels, mxu_levels):
    dk, dv = GLA_HEAD_DK, GLA_HEAD_DV
    heads = [slice(h * dk, (h + 1) * dk) for h in range(GLA_HEADS)]
    v_refs = (v01_ref, v01_ref, v23_ref, v23_ref)

    @pl.when(pl.program_id(1) == 0)
    def _():
        st_ref[...] = jnp.zeros_like(st_ref)

    rank_lanes = lax.broadcasted_iota(jnp.int32, (c, LANES), 1) < GATE_RANK
    pos = lax.broadcasted_iota(jnp.int32, (c, GLA_K_WIDTH), 0)
    row_bit = [((pos >> l) & 1) == 1 for l in range(mxu_levels)]

    def chunk(rows):
        lr = jnp.where(rank_lanes, lr_ref[rows, :], jnp.zeros((c, LANES), BF16))
        pre = _dot(lr, wup_ref[...]) + bgk_ref[...]
        g = (jnp.minimum(pre, 0.0) - jnp.log2(1.0 + jnp.exp2(-jnp.abs(pre)))) * (1.0 / GATE_TAU)
        g_hi, g_lo = _split_hi_lo(g)
        x = _dot(e2_ref[...], jnp.concatenate([g_hi, g_lo], axis=0))
        cum = x[:c]

        def level_exponent(l):
            if l < mxu_levels:
                return x[(l + 1) * c:(l + 2) * c]
            half = 1 << l
            parts = []
            for base in range(0, c, 2 * half):
                p = base + half
                ref = jnp.broadcast_to(cum[p - 1:p, :], (half, cum.shape[1]))
                parts.append(ref - cum[base:p])
                parts.append(cum[p:p + half] - ref)
            return jnp.concatenate(parts, axis=0)

        q = q_ref[rows, :].astype(F32)
        k = k_ref[rows, :].astype(F32)
        lvl = lvl_ref[...]

        def query_or_key(l):
            half = 1 << l
            if half >= SUBLANES:
                return jnp.concatenate(
                    [(q if (r0 // half) % 2 else k)[r0:r0 + half] for r0 in range(0, c, half)],
                    axis=0)
            return jnp.where(row_bit[l], q, k)

        q16, k16 = q.astype(BF16), k.astype(BF16)
        att = [jnp.where(lvl == levels, _dot_nt(q16[:, hs], k16[:, hs]), 0.0) for hs in heads]
        for l in range(levels):
            qk = (query_or_key(l) * jnp.exp2(level_exponent(l))).astype(BF16)
            att = [jnp.where(lvl == l, _dot_nt(qk[:, hs], qk[:, hs]), a)
                   for hs, a in zip(heads, att)]

        last = cum[c - 1:c, :]
        q_dec = (q * jnp.exp2(cum)).astype(BF16)
        k_dec = (k * jnp.exp2(last - cum)).astype(BF16)
        st_decay = jnp.exp2(last)
        outs = []
        for h, hs in enumerate(heads):
            v = v_refs[h][rows, (h % 2) * dv:(h % 2 + 1) * dv]
            st = st_ref[h]
            o = _dot(att[h].astype(BF16), v) + _dot_nt(q_dec[:, hs], st.astype(BF16))
            st_ref[h] = st * st_decay[:, hs] + lax.dot_general(
                v, k_dec[:, hs], (((0,), (0,)), ((), ())), preferred_element_type=F32)
            outs.append(_rms(o, gn_ref[...]))
        r = jnp.concatenate([r01_ref[rows, :], r23_ref[rows, :]], axis=1).astype(F32)
        o_ref[rows, :] = (jnp.concatenate(outs, axis=1) * (r * jax.nn.sigmoid(r))).astype(o_ref.dtype)

    for n in range(n_chunks):
        chunk(slice(n * c, (n + 1) * c))


def _gla(proj3, w_up_pad, b_gk, head_gain, tm):
    b, lp, _ = proj3.shape
    c, levels, mxu_levels = GLA_CHUNK, GLA_LEVELS, GLA_MXU_LEVELS
    e2, lvl = _gla_constants(c, levels, mxu_levels)
    kw, vw, vb = GLA_K_WIDTH, GLA_V_WIDTH, GLA_VBLOCK
    n_chunks = tm // c
    kern = functools.partial(_gla_kernel, c=c, n_chunks=n_chunks, levels=levels,
                             mxu_levels=mxu_levels)
    return pl.pallas_call(
        kern,
        grid=(b, lp // tm),
        in_specs=[
            pl.BlockSpec((None, tm, kw), lambda bi, t: (bi, t, COL_GQ // kw)),
            pl.BlockSpec((None, tm, kw), lambda bi, t: (bi, t, COL_GK // kw)),
            pl.BlockSpec((None, tm, vb), lambda bi, t: (bi, t, COL_GV // vb)),
            pl.BlockSpec((None, tm, vb), lambda bi, t: (bi, t, COL_GV // vb + 1)),
            pl.BlockSpec((None, tm, vb), lambda bi, t: (bi, t, COL_GR // vb)),
            pl.BlockSpec((None, tm, vb), lambda bi, t: (bi, t, COL_GR // vb + 1)),
            pl.BlockSpec((None, tm, LANES), lambda bi, t: (bi, t, COL_LR // LANES)),
            _resident((LANES, kw)),
            _resident((1, kw)),
            _resident((1, GLA_HEAD_DV)),
            _resident(((mxu_levels + 1) * c, 2 * c)),
            _resident((c, c)),
        ],
        out_specs=pl.BlockSpec((None, tm, vw), lambda bi, t: (bi, t, 0)),
        out_shape=jax.ShapeDtypeStruct((b, lp, vw), BF16),
        scratch_shapes=[pltpu.VMEM((GLA_HEADS, GLA_HEAD_DV, GLA_HEAD_DK), F32)],
        compiler_params=pltpu.CompilerParams(
            dimension_semantics=("arbitrary", "arbitrary"),
            vmem_limit_bytes=VMEM_LIMIT_BYTES),
        name="gla",
    )(proj3, proj3, proj3, proj3, proj3, proj3, proj3, w_up_pad, b_gk, head_gain, e2, lvl)


def _merge_kernel(x_ref, meta_ref, gpre_ref, wm_ref, sb_ref, og_ref, wsb_ref, wgla_ref, wo_ref,
                  gpost_ref, up_ref, gate_ref, down_ref, o_ref, up16_ref, gate16_ref, down16_ref,
                  h_ref, wm16_ref, *, tm):
    up16_ref[...] = up_ref[...].astype(BF16)
    gate16_ref[...] = gate_ref[...].astype(BF16)
    down16_ref[...] = down_ref[...].astype(BF16)

    @pl.when(jnp.logical_and(pl.program_id(0) == 0, pl.program_id(1) == 0))
    def _():
        for c0 in range(0, 2 * D_MODEL, MXU_WIDTH):
            cols = slice(c0, c0 + MXU_WIDTH)
            wm16_ref[:, cols] = wm_ref[cols, :].T.astype(BF16)

    h = _padded_rows(x_ref, meta_ref, h_ref, tm)
    hn = _rms(h, gpre_ref[...]).astype(BF16)
    m = _dot(hn, wm16_ref[...])
    a = _dot(sb_ref[...], wsb_ref[...])
    b = _dot(og_ref[...], wgla_ref[...])
    mixed = jax.nn.sigmoid(m[:, :D_MODEL]) * a + jax.nn.sigmoid(m[:, D_MODEL:]) * b
    mix = _dot(mixed.astype(BF16), wo_ref[...])
    o_ref[...] = h + _rms(mix, gpost_ref[...])


def _merge(x, meta, g_pre, w_in_t, sb, og, w_sb, w_gla, w_o, g_post, ffn_weights, layer, tm, lp):
    b = x.shape[0]
    n_tiles = lp // tm
    kern = functools.partial(_merge_kernel, tm=tm)
    merge_rows = pl.BlockSpec(
        (pl.Squeezed(), pl.Element(2 * D_MODEL), pl.Element(D_MODEL)),
        lambda bi, i: (0, COL_MERGE, 0), pipeline_mode=pl.Buffered(1))

    n_blocks = min(FFN_CAST_BLOCKS, (b * n_tiles) // len(ffn_weights))
    assert n_blocks >= 1
    cast_in, cast_out, cast_shapes = [], [], []
    for n, w in enumerate(ffn_weights):
        _, rows, cols = w.shape
        blk = rows // n_blocks

        def block(bi, i, n=n):
            return jnp.clip(bi * n_tiles + i - n * n_blocks, 0, n_blocks - 1)

        cast_in.append(pl.BlockSpec((None, blk, cols), lambda bi, i, f=block: (layer, f(bi, i), 0)))
        cast_out.append(pl.BlockSpec((blk, cols), lambda bi, i, f=block: (f(bi, i), 0)))
        cast_shapes.append(jax.ShapeDtypeStruct((rows, cols), BF16))

    return pl.pallas_call(
        kern,
        grid=(b, n_tiles),
        in_specs=[
            _x_window(tm),
            _resident((N_META, D_MODEL)),
            _resident((1, D_MODEL)),
            merge_rows,
            pl.BlockSpec((None, tm, SB_WIDTH), lambda bi, i: (bi, i, 0)),
            pl.BlockSpec((None, tm, GLA_V_WIDTH), lambda bi, i: (bi, i, 0)),
            _resident((SB_WIDTH, D_MODEL)),
            _resident((GLA_V_WIDTH, D_MODEL)),
            _resident((D_MODEL, D_MODEL)),
            _resident((1, D_MODEL)),
            *cast_in,
        ],
        out_specs=[pl.BlockSpec((None, tm, D_MODEL), lambda bi, i: (bi, i, 0)), *cast_out],
        out_shape=[jax.ShapeDtypeStruct((b, lp, D_MODEL), F32), *cast_shapes],
        scratch_shapes=[
            pltpu.VMEM((tm, D_MODEL), F32),
            pltpu.VMEM((D_MODEL, 2 * D_MODEL), BF16),
        ],
        compiler_params=pltpu.CompilerParams(
            dimension_semantics=("arbitrary", "arbitrary"), vmem_limit_bytes=VMEM_LIMIT_BYTES),
        name="merge",
    )(x, meta, g_pre, w_in_t, sb, og, w_sb, w_gla, w_o, g_post, *ffn_weights)


def _ffn_kernel(h_ref, hist_ref, gpre_ref, wup_ref, wgate_ref, cw_ref, cb_ref, wdown_ref,
                gpost_ref, o_ref, tail_ref, act_ref, *, tm):
    chunks = [slice(c0, c0 + FF_CHUNK) for c0 in range(0, D_FF, FF_CHUNK)]

    @pl.when(pl.program_id(1) == 0)
    def _():
        hist = _rms(hist_ref[...], gpre_ref[...]).astype(BF16)
        for cols in chunks:
            tail_ref[:, cols] = _dot(hist, wup_ref[:, cols])

    h = h_ref[...]
    hn = _rms(h, gpre_ref[...]).astype(BF16)
    row = lax.broadcasted_iota(jnp.int32, (SUBLANES, FF_CHUNK), 0)
    for cols in chunks:
        up = _dot(hn, wup_ref[:, cols])
        gate = _dot(hn, wgate_ref[:, cols])
        prev = tail_ref[:, cols]
        r1 = pltpu.roll(up, 1, 0)
        r2 = pltpu.roll(up, 2, 0)
        last, before_last = prev[SUBLANES - 1:SUBLANES, :], prev[SUBLANES - 2:SUBLANES - 1, :]
        head1 = jnp.where(row == 0, last, r1[:SUBLANES])
        head2 = jnp.where(row == 0, before_last, jnp.where(row == 1, last, r2[:SUBLANES]))
        up1 = jnp.concatenate([head1, r1[SUBLANES:]], axis=0)
        up2 = jnp.concatenate([head2, r2[SUBLANES:]], axis=0)
        tail_ref[:, cols] = up[tm - SUBLANES:, :]
        cw = cw_ref[:, cols]
        y = cw[0:1, :] * up2 + cw[1:2, :] * up1 + cw[2:3, :] * up + cb_ref[:, cols]
        act_ref[:, cols] = (jax.nn.gelu(y, approximate=True) * gate).astype(BF16)
    ffn = _dot(act_ref[...], wdown_ref[...])
    o_ref[...] = h + _rms(ffn, gpost_ref[...])


def _ffn(h3, g_pre, w_up, w_gate, conv_w, conv_b, w_down, g_post, tm):
    b, lp, _ = h3.shape
    s = lp - FRONT
    kern = functools.partial(_ffn_kernel, tm=tm)

    def rows(n, start):
        return pl.BlockSpec((pl.Squeezed(), pl.Element(n), pl.Element(D_MODEL)), start)

    return pl.pallas_call(
        kern,
        grid=(b, s // tm),
        in_specs=[
            rows(tm, lambda bi, i: (bi, pl.multiple_of(FRONT + i * tm, Q_BLOCK), 0)),
            rows(SUBLANES, lambda bi, i: (bi, FRONT - SUBLANES, 0)),
            _resident((1, D_MODEL)),
            _resident((D_MODEL, D_FF)),
            _resident((D_MODEL, D_FF)),
            _resident((CONV_W, D_FF)),
            _resident((1, D_FF)),
            _resident((D_FF, D_MODEL)),
            _resident((1, D_MODEL)),
        ],
        out_specs=pl.BlockSpec((None, tm, D_MODEL), lambda bi, i: (bi, i, 0)),
        out_shape=jax.ShapeDtypeStruct((b, s, D_MODEL), F32),
        scratch_shapes=[
            pltpu.VMEM((SUBLANES, D_FF), F32),
            pltpu.VMEM((tm, D_FF), BF16),
        ],
        compiler_params=pltpu.CompilerParams(
            dimension_semantics=("arbitrary", "arbitrary"), vmem_limit_bytes=VMEM_LIMIT_BYTES),
        name="ffn",
    )(h3, h3, g_pre, w_up, w_gate, conv_w, conv_b, w_down, g_post)


def _pick_tile(n, want):
    t = want
    while n % t:
        t -= Q_BLOCK
    return t


def kernel(x, meta_tokens, norm_mix_pre, w_in, w_gk_up, b_gk, gla_head_norm, w_sb_out, w_gla_out, w_o,
           norm_mix_post, norm_ffn_pre, w_ffn_up, w_ffn_gate, conv_w, conv_b, w_ffn_down, norm_ffn_post):
    b, s, _ = x.shape
    assert s % Q_BLOCK == 0 and norm_mix_pre.shape[0] == 1
    lp = FRONT + s
    tm = _pick_tile(lp, ROW_TILE)
    tq = _pick_tile(lp, SB_TQ)
    tf = _pick_tile(s, FFN_TILE)
    d = 0

    w_in_t = jnp.swapaxes(w_in, 1, 2)
    w_up_pad = jnp.pad((w_gk_up[d] * LOG2_E).astype(BF16), ((0, LANES - GATE_RANK), (0, 0)))
    b_gk2 = b_gk[d][None, :] * LOG2_E

    proj3 = _inproj(x, meta_tokens, norm_mix_pre[d][None, :], w_in_t, tm, lp)
    sb = _stick_breaking(proj3, tq, SB_TK)
    og = _gla(proj3, w_up_pad, b_gk2, gla_head_norm[d][None, :], tm)
    h1, w_up16, w_gate16, w_down16 = _merge(
        x, meta_tokens, norm_mix_pre[d][None, :], w_in_t, sb, og,
        w_sb_out[d].astype(BF16), w_gla_out[d].astype(BF16), w_o[d].astype(BF16),
        norm_mix_post[d][None, :], (w_ffn_up, w_ffn_gate, w_ffn_down), d, tm, lp)
    return _ffn(h1, norm_ffn_pre[d][None, :], w_up16, w_gate16, conv_w[d], conv_b[d][None, :],
                w_down16, norm_ffn_post[d][None, :], tf)
```

```python
import functools

import numpy as np
import jax
import jax.numpy as jnp
from jax import lax
from jax.experimental import pallas as pl
from jax.experimental.pallas import tpu as pltpu

F32 = jnp.float32
BF16 = jnp.bfloat16

D_MODEL = 1024
N_META = 16
Q_BLOCK = 128
SB_HEADS = 8
SB_HEAD_DIM = 64
GLA_HEADS = 4
GLA_HEAD_DK = 128
GLA_HEAD_DV = 256
GATE_RANK = 16
GATE_TAU = 16.0
D_FF = 2816
CONV_W = 3
EPS = 1e-6

SB_WIDTH = SB_HEADS * SB_HEAD_DIM
GLA_K_WIDTH = GLA_HEADS * GLA_HEAD_DK
GLA_V_WIDTH = GLA_HEADS * GLA_HEAD_DV

LANES = 128
SUBLANES = 8
MXU_WIDTH = 256
VMEM_LIMIT_BYTES = 56 * 1024 * 1024

COL_SBQ = 0
COL_SBK = COL_SBQ + SB_WIDTH
COL_SBV = COL_SBK + SB_WIDTH
COL_GQ = COL_SBV + SB_WIDTH
COL_GK = COL_GQ + GLA_K_WIDTH
COL_GV = COL_GK + GLA_K_WIDTH
COL_GR = COL_GV + GLA_V_WIDTH
COL_LR = COL_GR + GLA_V_WIDTH
COL_MERGE = COL_LR + GATE_RANK
PROJ_WIDTH = COL_LR + LANES

FRONT = Q_BLOCK
ROW_TILE = 640
FFN_TILE = 1024
SB_TQ = 640
SB_TK = 128
GLA_CHUNK = 128
GLA_LEVELS = GLA_CHUNK.bit_length() - 1
GLA_MXU_LEVELS = SUBLANES.bit_length() - 1
FF_CHUNK = 256
FFN_CAST_BLOCKS = 8
LOG2_E = 1.4426950408889634
SB_Q_SCALE = SB_HEAD_DIM ** -0.5 * LOG2_E
GLA_Q_SCALE = GLA_HEAD_DK ** -0.5
SB_LOG2_UNDERFLOW = -126.0
SB_MASKED = -1e30


def _dot(a, b):
    return jnp.dot(a, b, preferred_element_type=F32)


def _dot_nt(a, b):
    return lax.dot_general(a, b, (((1,), (1,)), ((), ())), preferred_element_type=F32)


def _rms(x, gain):
    ms = jnp.mean(x * x, axis=-1, keepdims=True)
    return x * lax.rsqrt(ms + EPS) * gain


def _split_hi_lo(x):
    hi = x.astype(BF16)
    lo = (x - hi.astype(F32)).astype(BF16)
    return hi, lo


def _resident(shape):
    nd = len(shape)
    return pl.BlockSpec(shape, lambda *_: (0,) * nd, pipeline_mode=pl.Buffered(1))


def _x_window(tm):
    return pl.BlockSpec(
        (pl.Squeezed(), pl.Element(tm), pl.Element(D_MODEL)),
        lambda b, i: (b, pl.multiple_of(jnp.maximum(i * tm - FRONT, 0), Q_BLOCK), 0))


def _padded_rows(x_ref, meta_ref, h_ref, tm):
    i = pl.program_id(1)

    @pl.when(i == 0)
    def _():
        h_ref[0:FRONT - N_META, :] = jnp.zeros((FRONT - N_META, D_MODEL), F32)
        h_ref[FRONT - N_META:FRONT, :] = meta_ref[...]
        h_ref[FRONT:, :] = x_ref[0:tm - FRONT, :]

    @pl.when(i > 0)
    def _():
        h_ref[...] = x_ref[...]

    return h_ref[...]


def _inproj_kernel(x_ref, meta_ref, g_ref, w_ref, o_ref, h_ref, w16_ref, *, tm):
    chunks = [slice(c0, min(c0 + MXU_WIDTH, PROJ_WIDTH)) for c0 in range(0, PROJ_WIDTH, MXU_WIDTH)]

    def column_scale(cols):
        if COL_SBQ <= cols.start < COL_SBQ + SB_WIDTH:
            return SB_Q_SCALE
        if COL_GQ <= cols.start < COL_GQ + GLA_K_WIDTH:
            return GLA_Q_SCALE
        return None

    @pl.when(jnp.logical_and(pl.program_id(0) == 0, pl.program_id(1) == 0))
    def _():
        for cols in chunks:
            w = w_ref[cols, :].T
            scale = column_scale(cols)
            w16_ref[:, cols] = (w if scale is None else w * scale).astype(BF16)

    h = _padded_rows(x_ref, meta_ref, h_ref, tm)
    hn = _rms(h, g_ref[...]).astype(BF16)
    for cols in chunks:
        o_ref[:, cols] = _dot(hn, w16_ref[:, cols]).astype(BF16)


def _inproj(x, meta, gain, w_in_t, tm, lp):
    b = x.shape[0]
    kern = functools.partial(_inproj_kernel, tm=tm)
    return pl.pallas_call(
        kern,
        grid=(b, lp // tm),
        in_specs=[
            _x_window(tm),
            _resident((N_META, D_MODEL)),
            _resident((1, D_MODEL)),
            pl.BlockSpec((None, PROJ_WIDTH, D_MODEL), lambda bi, i: (0, 0, 0),
                         pipeline_mode=pl.Buffered(1)),
        ],
        out_specs=pl.BlockSpec((None, tm, PROJ_WIDTH), lambda bi, i: (bi, i, 0)),
        out_shape=jax.ShapeDtypeStruct((b, lp, PROJ_WIDTH), BF16),
        scratch_shapes=[
            pltpu.VMEM((tm, D_MODEL), F32),
            pltpu.VMEM((D_MODEL, PROJ_WIDTH), BF16),
        ],
        compiler_params=pltpu.CompilerParams(
            dimension_semantics=("arbitrary", "arbitrary"), vmem_limit_bytes=VMEM_LIMIT_BYTES),
        name="inproj",
    )(x, meta, gain, w_in_t)


def _sb_tail_matrix(tk):
    j = np.arange(tk)
    later = (j[:, None] > j[None, :]).astype(np.float32)
    ones, zero = np.ones((tk, tk), np.float32), np.zeros((tk, tk), np.float32)
    return jnp.asarray(-np.block([[later, zero, ones, zero], [zero, later, zero, ones]]), dtype=BF16)


def _sb_kernel(q_ref, k_ref, v_ref, tt_ref, o_ref, acc_ref, c_ref, *, tq, tk):
    i = pl.program_id(2)
    nd = tq // tk
    q = q_ref[...]
    tt = tt_ref[...]
    head_a = lax.broadcasted_iota(jnp.int32, (tk, 2 * SB_HEAD_DIM), 1) < SB_HEAD_DIM
    col = lax.broadcasted_iota(jnp.int32, (tk, 2 * tk), 1)
    key = jnp.where(col >= tk, col - tk, col)
    row = lax.broadcasted_iota(jnp.int32, (tk, 2 * tk), 0)
    causal_bias = jnp.where(key < row, 0.0, SB_MASKED)

    def block_diag(x):
        zero = jnp.zeros_like(x)
        return jnp.concatenate([jnp.where(head_a, x, zero), jnp.where(head_a, zero, x)], axis=0)

    def step(delta, diagonal):
        zs, v_bds = [], []
        for r in range(nd):
            jb = i * nd + r - delta
            start = pl.multiple_of(jnp.maximum(jb, 0) * tk, tk)
            k_bd = block_diag(k_ref[pl.ds(start, tk), :])
            v_bds.append(block_diag(v_ref[pl.ds(start, tk), :]))
            z = _dot_nt(q[r * tk:(r + 1) * tk, :], k_bd)
            first_key = jnp.where(jb == 0, FRONT - N_META, jnp.where(jb < 0, tk, 0))
            z = z + jnp.where(key[0:1, :] >= first_key, 0.0, SB_MASKED)
            if diagonal:
                z = z + causal_bias
            zs.append(z)
        z = jnp.concatenate(zs, axis=0)
        s = jnp.maximum(z, 0.0) + jnp.log2(1.0 + jnp.exp2(-jnp.abs(z)))
        log_beta = z - s
        sums = _dot(s.astype(BF16), tt)
        carry = c_ref[...]
        tail = sums[:, :2 * tk] + carry
        w = jnp.exp2(log_beta + tail).astype(BF16)
        carry = carry + sums[:, 2 * tk:]
        c_ref[...] = carry
        acc_ref[...] += jnp.concatenate(
            [_dot(w[r * tk:(r + 1) * tk, :], v_bds[r]) for r in range(nd)], axis=0)
        return jnp.max(jnp.max(carry, axis=0, keepdims=True), axis=1, keepdims=True)[0, 0]

    acc_ref[...] = jnp.zeros_like(acc_ref)
    c_ref[...] = jnp.zeros_like(c_ref)
    step(0, True)
    stick = step(1, False)

    def live(state):
        delta, stick = state
        return jnp.logical_and(delta < (i + 1) * nd, stick > SB_LOG2_UNDERFLOW)

    lax.while_loop(live, lambda state: (state[0] + 1, step(state[0], False)), (2, stick))
    o_ref[...] = acc_ref[...].astype(o_ref.dtype)


def _stick_breaking(proj3, tq, tk):
    b, lp, _ = proj3.shape
    pair = 2 * SB_HEAD_DIM
    n_pairs = SB_WIDTH // pair
    kern = functools.partial(_sb_kernel, tq=tq, tk=tk)
    return pl.pallas_call(
        kern,
        grid=(b, n_pairs, lp // tq),
        in_specs=[
            pl.BlockSpec((None, tq, pair), lambda bi, hp, i: (bi, i, COL_SBQ // pair + hp)),
            pl.BlockSpec((None, lp, pair), lambda bi, hp, i: (bi, 0, COL_SBK // pair + hp)),
            pl.BlockSpec((None, lp, pair), lambda bi, hp, i: (bi, 0, COL_SBV // pair + hp)),
            _resident((2 * tk, 4 * tk)),
        ],
        out_specs=pl.BlockSpec((None, tq, pair), lambda bi, hp, i: (bi, i, hp)),
        out_shape=jax.ShapeDtypeStruct((b, lp, SB_WIDTH), BF16),
        scratch_shapes=[
            pltpu.VMEM((tq, pair), F32),
            pltpu.VMEM((tq, 2 * tk), F32),
        ],
        compiler_params=pltpu.CompilerParams(
            dimension_semantics=("arbitrary", "arbitrary", "arbitrary"),
            vmem_limit_bytes=VMEM_LIMIT_BYTES),
        name="stick_breaking",
    )(proj3, proj3, proj3, _sb_tail_matrix(tk))


def _gla_constants(c, levels, mxu_levels):
    t = np.arange(c)[:, None]
    j = np.arange(c)[None, :]
    blocks = [(j <= t)]
    for l in range(mxu_levels):
        half = 1 << l
        p = (t >> (l + 1) << (l + 1)) + half
        upper = ((t >> l) & 1) == 1
        blocks.append(np.where(upper, (j >= p) & (j <= t), (j > t) & (j < p)))
    e = np.concatenate(blocks, axis=0).astype(np.float32)
    e2 = np.concatenate([e, e], axis=1)
    s = np.arange(c)[None, :]
    x = t ^ s
    lvl = np.where(x > 0, np.floor(np.log2(np.maximum(x, 1))).astype(np.int32), levels)
    lvl = np.where(s > t, -1, lvl).astype(np.int32)
    return jnp.asarray(e2, dtype=BF16), jnp.asarray(lvl)


def _gla_kernel(in_ref, wup_ref, bgk_ref, gn_ref, e2_ref, lvl_ref, o_ref, st_ref, *, c, n_chunks,
                levels, mxu_levels):
    dk, dv = GLA_HEAD_DK, GLA_HEAD_DV
    heads = [slice(h * dk, (h + 1) * dk) for h in range(GLA_HEADS)]
    q_cols = slice(COL_GQ - COL_GQ, COL_GK - COL_GQ)
    k_cols = slice(COL_GK - COL_GQ, COL_GV - COL_GQ)
    r_cols = slice(COL_GR - COL_GQ, COL_LR - COL_GQ)
    lr_cols = slice(COL_LR - COL_GQ, COL_LR - COL_GQ + LANES)

    @pl.when(pl.program_id(1) == 0)
    def _():
        st_ref[...] = jnp.zeros_like(st_ref)

    rank_lanes = lax.broadcasted_iota(jnp.int32, (c, LANES), 1) < GATE_RANK
    pos = lax.broadcasted_iota(jnp.int32, (c, GLA_K_WIDTH), 0)
    row_bit = [((pos >> l) & 1) == 1 for l in range(mxu_levels)]

    def chunk(rows):
        lr = jnp.where(rank_lanes, in_ref[rows, lr_cols], jnp.zeros((c, LANES), BF16))
        pre = _dot(lr, wup_ref[...]) + bgk_ref[...]
        g = (jnp.minimum(pre, 0.0) - jnp.log2(1.0 + jnp.exp2(-jnp.abs(pre)))) * (1.0 / GATE_TAU)
        g_hi, g_lo = _split_hi_lo(g)
        x = _dot(e2_ref[...], jnp.concatenate([g_hi, g_lo], axis=0))
        cum = x[:c]

        def level_exponent(l):
            if l < mxu_levels:
                return x[(l + 1) * c:(l + 2) * c]
            half = 1 << l
            parts = []
            for base in range(0, c, 2 * half):
                p = base + half
                ref = jnp.broadcast_to(cum[p - 1:p, :], (half, cum.shape[1]))
                parts.append(ref - cum[base:p])
                parts.append(cum[p:p + half] - ref)
            return jnp.concatenate(parts, axis=0)

        q = in_ref[rows, q_cols].astype(F32)
        k = in_ref[rows, k_cols].astype(F32)
        lvl = lvl_ref[...]

        def query_or_key(l):
            half = 1 << l
            if half >= SUBLANES:
                return jnp.concatenate(
                    [(q if (r0 // half) % 2 else k)[r0:r0 + half] for r0 in range(0, c, half)],
                    axis=0)
            return jnp.where(row_bit[l], q, k)

        q16, k16 = q.astype(BF16), k.astype(BF16)
        att = [jnp.where(lvl == levels, _dot_nt(q16[:, hs], k16[:, hs]), 0.0) for hs in heads]
        for l in range(levels):
            qk = (query_or_key(l) * jnp.exp2(level_exponent(l))).astype(BF16)
            att = [jnp.where(lvl == l, _dot_nt(qk[:, hs], qk[:, hs]), a)
                   for hs, a in zip(heads, att)]

        last = cum[c - 1:c, :]
        q_dec = (q * jnp.exp2(cum)).astype(BF16)
        k_dec = (k * jnp.exp2(last - cum)).astype(BF16)
        st_decay = jnp.exp2(last)
        outs = []
        for h, hs in enumerate(heads):
            v = in_ref[rows, COL_GV - COL_GQ + h * dv:COL_GV - COL_GQ + (h + 1) * dv]
            st = st_ref[h]
            o = _dot(att[h].astype(BF16), v) + _dot_nt(q_dec[:, hs], st.astype(BF16))
            st_ref[h] = st * st_decay[:, hs] + lax.dot_general(
                v, k_dec[:, hs], (((0,), (0,)), ((), ())), preferred_element_type=F32)
            outs.append(_rms(o, gn_ref[...]))
        r = in_ref[rows, r_cols].astype(F32)
        o_ref[rows, :] = (jnp.concatenate(outs, axis=1) * (r * jax.nn.sigmoid(r))).astype(o_ref.dtype)

    for n in range(n_chunks):
        chunk(slice(n * c, (n + 1) * c))


def _gla(proj3, w_up_pad, b_gk, head_gain, tm):
    b, lp, _ = proj3.shape
    c, levels, mxu_levels = GLA_CHUNK, GLA_LEVELS, GLA_MXU_LEVELS
    e2, lvl = _gla_constants(c, levels, mxu_levels)
    kw, vw = GLA_K_WIDTH, GLA_V_WIDTH
    n_chunks = tm // c
    kern = functools.partial(_gla_kernel, c=c, n_chunks=n_chunks, levels=levels,
                             mxu_levels=mxu_levels)
    return pl.pallas_call(
        kern,
        grid=(b, lp // tm),
        in_specs=[
            pl.BlockSpec(
                (pl.Squeezed(), pl.Element(tm), pl.Element(PROJ_WIDTH - COL_GQ)),
                lambda bi, t: (bi, pl.multiple_of(t * tm, Q_BLOCK), COL_GQ)),
            _resident((LANES, kw)),
            _resident((1, kw)),
            _resident((1, GLA_HEAD_DV)),
            _resident(((mxu_levels + 1) * c, 2 * c)),
            _resident((c, c)),
        ],
        out_specs=pl.BlockSpec((None, tm, vw), lambda bi, t: (bi, t, 0)),
        out_shape=jax.ShapeDtypeStruct((b, lp, vw), BF16),
        scratch_shapes=[pltpu.VMEM((GLA_HEADS, GLA_HEAD_DV, GLA_HEAD_DK), F32)],
        compiler_params=pltpu.CompilerParams(
            dimension_semantics=("arbitrary", "arbitrary"),
            vmem_limit_bytes=VMEM_LIMIT_BYTES),
        name="gla",
    )(proj3, w_up_pad, b_gk, head_gain, e2, lvl)


def _merge_kernel(x_ref, meta_ref, gpre_ref, wm_ref, sb_ref, og_ref, wsb_ref, wgla_ref, wo_ref,
                  gpost_ref, up_ref, gate_ref, down_ref, o_ref, up16_ref, gate16_ref, down16_ref,
                  h_ref, wm16_ref, *, tm):
    up16_ref[...] = up_ref[...].astype(BF16)
    gate16_ref[...] = gate_ref[...].astype(BF16)
    down16_ref[...] = down_ref[...].astype(BF16)

    @pl.when(jnp.logical_and(pl.program_id(0) == 0, pl.program_id(1) == 0))
    def _():
        for c0 in range(0, 2 * D_MODEL, MXU_WIDTH):
            cols = slice(c0, c0 + MXU_WIDTH)
            wm16_ref[:, cols] = wm_ref[cols, :].T.astype(BF16)

    h = _padded_rows(x_ref, meta_ref, h_ref, tm)
    hn = _rms(h, gpre_ref[...]).astype(BF16)
    m = _dot(hn, wm16_ref[...])
    a = _dot(sb_ref[...], wsb_ref[...])
    b = _dot(og_ref[...], wgla_ref[...])
    mixed = jax.nn.sigmoid(m[:, :D_MODEL]) * a + jax.nn.sigmoid(m[:, D_MODEL:]) * b
    mix = _dot(mixed.astype(BF16), wo_ref[...])
    o_ref[...] = h + _rms(mix, gpost_ref[...])


def _merge(x, meta, g_pre, w_in_t, sb, og, w_sb, w_gla, w_o, g_post, ffn_weights, layer, tm, lp):
    b = x.shape[0]
    n_tiles = lp // tm
    kern = functools.partial(_merge_kernel, tm=tm)
    merge_rows = pl.BlockSpec(
        (pl.Squeezed(), pl.Element(2 * D_MODEL), pl.Element(D_MODEL)),
        lambda bi, i: (0, COL_MERGE, 0), pipeline_mode=pl.Buffered(1))

    n_blocks = min(FFN_CAST_BLOCKS, (b * n_tiles) // len(ffn_weights))
    assert n_blocks >= 1
    cast_in, cast_out, cast_shapes = [], [], []
    for n, w in enumerate(ffn_weights):
        _, rows, cols = w.shape
        blk = rows // n_blocks

        def block(bi, i, n=n):
            return jnp.clip(bi * n_tiles + i - n * n_blocks, 0, n_blocks - 1)

        cast_in.append(pl.BlockSpec((None, blk, cols), lambda bi, i, f=block: (layer, f(bi, i), 0)))
        cast_out.append(pl.BlockSpec((blk, cols), lambda bi, i, f=block: (f(bi, i), 0)))
        cast_shapes.append(jax.ShapeDtypeStruct((rows, cols), BF16))

    return pl.pallas_call(
        kern,
        grid=(b, n_tiles),
        in_specs=[
            _x_window(tm),
            _resident((N_META, D_MODEL)),
            _resident((1, D_MODEL)),
            merge_rows,
            pl.BlockSpec((None, tm, SB_WIDTH), lambda bi, i: (bi, i, 0)),
            pl.BlockSpec((None, tm, GLA_V_WIDTH), lambda bi, i: (bi, i, 0)),
            _resident((SB_WIDTH, D_MODEL)),
            _resident((GLA_V_WIDTH, D_MODEL)),
            _resident((D_MODEL, D_MODEL)),
            _resident((1, D_MODEL)),
            *cast_in,
        ],
        out_specs=[pl.BlockSpec((None, tm, D_MODEL), lambda bi, i: (bi, i, 0)), *cast_out],
        out_shape=[jax.ShapeDtypeStruct((b, lp, D_MODEL), F32), *cast_shapes],
        scratch_shapes=[
            pltpu.VMEM((tm, D_MODEL), F32),
            pltpu.VMEM((D_MODEL, 2 * D_MODEL), BF16),
        ],
        compiler_params=pltpu.CompilerParams(
            dimension_semantics=("arbitrary", "arbitrary"), vmem_limit_bytes=VMEM_LIMIT_BYTES),
        name="merge",
    )(x, meta, g_pre, w_in_t, sb, og, w_sb, w_gla, w_o, g_post, *ffn_weights)


def _ffn_kernel(h_ref, hist_ref, gpre_ref, wup_ref, wgate_ref, cw_ref, cb_ref, wdown_ref,
                gpost_ref, o_ref, tail_ref, act_ref, *, tm):
    chunks = [slice(c0, c0 + FF_CHUNK) for c0 in range(0, D_FF, FF_CHUNK)]

    @pl.when(pl.program_id(1) == 0)
    def _():
        hist = _rms(hist_ref[...], gpre_ref[...]).astype(BF16)
        for cols in chunks:
            tail_ref[:, cols] = _dot(hist, wup_ref[:, cols])

    h = h_ref[...]
    hn = _rms(h, gpre_ref[...]).astype(BF16)
    row = lax.broadcasted_iota(jnp.int32, (SUBLANES, FF_CHUNK), 0)
    for cols in chunks:
        up = _dot(hn, wup_ref[:, cols])
        gate = _dot(hn, wgate_ref[:, cols])
        prev = tail_ref[:, cols]
        r1 = pltpu.roll(up, 1, 0)
        r2 = pltpu.roll(up, 2, 0)
        last, before_last = prev[SUBLANES - 1:SUBLANES, :], prev[SUBLANES - 2:SUBLANES - 1, :]
        head1 = jnp.where(row == 0, last, r1[:SUBLANES])
        head2 = jnp.where(row == 0, before_last, jnp.where(row == 1, last, r2[:SUBLANES]))
        up1 = jnp.concatenate([head1, r1[SUBLANES:]], axis=0)
        up2 = jnp.concatenate([head2, r2[SUBLANES:]], axis=0)
        tail_ref[:, cols] = up[tm - SUBLANES:, :]
        cw = cw_ref[:, cols]
        y = cw[0:1, :] * up2 + cw[1:2, :] * up1 + cw[2:3, :] * up + cb_ref[:, cols]
        act_ref[:, cols] = (jax.nn.gelu(y, approximate=True) * gate).astype(BF16)
    ffn = _dot(act_ref[...], wdown_ref[...])
    o_ref[...] = h + _rms(ffn, gpost_ref[...])


def _ffn(h3, g_pre, w_up, w_gate, conv_w, conv_b, w_down, g_post, tm):
    b, lp, _ = h3.shape
    s = lp - FRONT
    kern = functools.partial(_ffn_kernel, tm=tm)

    def rows(n, start):
        return pl.BlockSpec((pl.Squeezed(), pl.Element(n), pl.Element(D_MODEL)), start)

    return pl.pallas_call(
        kern,
        grid=(b, s // tm),
        in_specs=[
            rows(tm, lambda bi, i: (bi, pl.multiple_of(FRONT + i * tm, Q_BLOCK), 0)),
            rows(SUBLANES, lambda bi, i: (bi, FRONT - SUBLANES, 0)),
            _resident((1, D_MODEL)),
            _resident((D_MODEL, D_FF)),
            _resident((D_MODEL, D_FF)),
            _resident((CONV_W, D_FF)),
            _resident((1, D_FF)),
            _resident((D_FF, D_MODEL)),
            _resident((1, D_MODEL)),
        ],
        out_specs=pl.BlockSpec((None, tm, D_MODEL), lambda bi, i: (bi, i, 0)),
        out_shape=jax.ShapeDtypeStruct((b, s, D_MODEL), F32),
        scratch_shapes=[
            pltpu.VMEM((SUBLANES, D_FF), F32),
            pltpu.VMEM((tm, D_FF), BF16),
        ],
        compiler_params=pltpu.CompilerParams(
            dimension_semantics=("arbitrary", "arbitrary"), vmem_limit_bytes=VMEM_LIMIT_BYTES),
        name="ffn",
    )(h3, h3, g_pre, w_up, w_gate, conv_w, conv_b, w_down, g_post)


def _pick_tile(n, want):
    t = want
    while n % t:
        t -= Q_BLOCK
    return t


def kernel(x, meta_tokens, norm_mix_pre, w_in, w_gk_up, b_gk, gla_head_norm, w_sb_out, w_gla_out, w_o,
           norm_mix_post, norm_ffn_pre, w_ffn_up, w_ffn_gate, conv_w, conv_b, w_ffn_down, norm_ffn_post):
    b, s, _ = x.shape
    assert s % Q_BLOCK == 0 and norm_mix_pre.shape[0] == 1
    lp = FRONT + s
    tm = _pick_tile(lp, ROW_TILE)
    tq = _pick_tile(lp, SB_TQ)
    tf = _pick_tile(s, FFN_TILE)
    d = 0

    w_in_t = jnp.swapaxes(w_in, 1, 2)
    w_up_pad = jnp.pad((w_gk_up[d] * LOG2_E).astype(BF16), ((0, LANES - GATE_RANK), (0, 0)))
    b_gk2 = b_gk[d][None, :] * LOG2_E

    proj3 = _inproj(x, meta_tokens, norm_mix_pre[d][None, :], w_in_t, tm, lp)
    sb = _stick_breaking(proj3, tq, SB_TK)
    og = _gla(proj3, w_up_pad, b_gk2, gla_head_norm[d][None, :], tm)
    h1, w_up16, w_gate16, w_down16 = _merge(
        x, meta_tokens, norm_mix_pre[d][None, :], w_in_t, sb, og,
        w_sb_out[d].astype(BF16), w_gla_out[d].astype(BF16), w_o[d].astype(BF16),
        norm_mix_post[d][None, :], (w_ffn_up, w_ffn_gate, w_ffn_down), d, tm, lp)
    return _ffn(h1, norm_ffn_pre[d][None, :], w_up16, w_gate16, conv_w[d], conv_b[d][None, :],
                w_down16, norm_ffn_post[d][None, :], tf)
```

```python
import functools

import numpy as np
import jax
import jax.numpy as jnp
from jax import lax
from jax.experimental import pallas as pl
from jax.experimental.pallas import tpu as pltpu

F32 = jnp.float32
BF16 = jnp.bfloat16

D_MODEL = 1024
N_META = 16
Q_BLOCK = 128
SB_HEADS = 8
SB_HEAD_DIM = 64
GLA_HEADS = 4
GLA_HEAD_DK = 128
GLA_HEAD_DV = 256
GATE_RANK = 16
GATE_TAU = 16.0
D_FF = 2816
CONV_W = 3
EPS = 1e-6

SB_WIDTH = SB_HEADS * SB_HEAD_DIM
GLA_K_WIDTH = GLA_HEADS * GLA_HEAD_DK
GLA_V_WIDTH = GLA_HEADS * GLA_HEAD_DV

LANES = 128
SUBLANES = 8
MXU_WIDTH = 256
VMEM_LIMIT_BYTES = 56 * 1024 * 1024

COL_SBQ = 0
COL_SBK = COL_SBQ + SB_WIDTH
COL_SBV = COL_SBK + SB_WIDTH
COL_GQ = COL_SBV + SB_WIDTH
COL_GK = COL_GQ + GLA_K_WIDTH
COL_GV = COL_GK + GLA_K_WIDTH
COL_GR = COL_GV + GLA_V_WIDTH
COL_LR = COL_GR + GLA_V_WIDTH
COL_MERGE = COL_LR + GATE_RANK
PROJ_WIDTH = COL_LR + LANES

FRONT = Q_BLOCK
ROW_TILE = 640
FFN_TILE = 1024
SB_TQ = 640
SB_TK = 128
GLA_CHUNK = 128
GLA_LEVELS = GLA_CHUNK.bit_length() - 1
GLA_MXU_LEVELS = SUBLANES.bit_length() - 1
FF_CHUNK = 256
FFN_CAST_BLOCKS = 8
LOG2_E = 1.4426950408889634
SB_Q_SCALE = SB_HEAD_DIM ** -0.5 * LOG2_E
GLA_Q_SCALE = GLA_HEAD_DK ** -0.5
SB_LOG2_UNDERFLOW = -126.0
SB_MASKED = -1e30


def _dot(a, b):
    return jnp.dot(a, b, preferred_element_type=F32)


def _dot_nt(a, b):
    return lax.dot_general(a, b, (((1,), (1,)), ((), ())), preferred_element_type=F32)


def _rms(x, gain):
    ms = jnp.mean(x * x, axis=-1, keepdims=True)
    return x * lax.rsqrt(ms + EPS) * gain


def _split_hi_lo(x):
    hi = x.astype(BF16)
    lo = (x - hi.astype(F32)).astype(BF16)
    return hi, lo


def _resident(shape):
    nd = len(shape)
    return pl.BlockSpec(shape, lambda *_: (0,) * nd, pipeline_mode=pl.Buffered(1))


def _x_window(tm):
    return pl.BlockSpec(
        (pl.Squeezed(), pl.Element(tm), pl.Element(D_MODEL)),
        lambda b, i: (b, pl.multiple_of(jnp.maximum(i * tm - FRONT, 0), Q_BLOCK), 0))


def _padded_rows(x_ref, meta_ref, h_ref, tm):
    i = pl.program_id(1)

    @pl.when(i == 0)
    def _():
        h_ref[0:FRONT - N_META, :] = jnp.zeros((FRONT - N_META, D_MODEL), F32)
        h_ref[FRONT - N_META:FRONT, :] = meta_ref[...]
        h_ref[FRONT:, :] = x_ref[0:tm - FRONT, :]

    @pl.when(i > 0)
    def _():
        h_ref[...] = x_ref[...]

    return h_ref[...]


def _inproj_kernel(x_ref, meta_ref, g_ref, w_ref, o_ref, h_ref, w16_ref, *, tm):
    chunks = [slice(c0, min(c0 + MXU_WIDTH, PROJ_WIDTH)) for c0 in range(0, PROJ_WIDTH, MXU_WIDTH)]

    def column_scale(cols):
        if COL_SBQ <= cols.start < COL_SBQ + SB_WIDTH:
            return SB_Q_SCALE
        if COL_GQ <= cols.start < COL_GQ + GLA_K_WIDTH:
            return GLA_Q_SCALE
        return None

    @pl.when(jnp.logical_and(pl.program_id(0) == 0, pl.program_id(1) == 0))
    def _():
        for cols in chunks:
            w = w_ref[cols, :].T
            scale = column_scale(cols)
            w16_ref[:, cols] = (w if scale is None else w * scale).astype(BF16)

    h = _padded_rows(x_ref, meta_ref, h_ref, tm)
    hn = _rms(h, g_ref[...]).astype(BF16)
    for cols in chunks:
        o_ref[:, cols] = _dot(hn, w16_ref[:, cols]).astype(BF16)


def _inproj(x, meta, gain, w_in_t, tm, lp):
    b = x.shape[0]
    kern = functools.partial(_inproj_kernel, tm=tm)
    return pl.pallas_call(
        kern,
        grid=(b, lp // tm),
        in_specs=[
            _x_window(tm),
            _resident((N_META, D_MODEL)),
            _resident((1, D_MODEL)),
            pl.BlockSpec((None, PROJ_WIDTH, D_MODEL), lambda bi, i: (0, 0, 0),
                         pipeline_mode=pl.Buffered(1)),
        ],
        out_specs=pl.BlockSpec((None, tm, PROJ_WIDTH), lambda bi, i: (bi, i, 0)),
        out_shape=jax.ShapeDtypeStruct((b, lp, PROJ_WIDTH), BF16),
        scratch_shapes=[
            pltpu.VMEM((tm, D_MODEL), F32),
            pltpu.VMEM((D_MODEL, PROJ_WIDTH), BF16),
        ],
        compiler_params=pltpu.CompilerParams(
            dimension_semantics=("arbitrary", "arbitrary"), vmem_limit_bytes=VMEM_LIMIT_BYTES),
        name="inproj",
    )(x, meta, gain, w_in_t)


def _sb_tail_matrix(tk):
    j = np.arange(tk)
    later = (j[:, None] > j[None, :]).astype(np.float32)
    return jnp.asarray(-np.concatenate([later, np.ones((tk, tk), np.float32)], axis=1), dtype=BF16)


def _sb_kernel(q_ref, k_ref, v_ref, tt_ref, o_ref, acc_ref, c_ref, *, tq, tk):
    i = pl.program_id(2)
    nd = tq // tk
    q = q_ref[...]
    tt = tt_ref[...]
    head_a = lax.broadcasted_iota(jnp.int32, (tk, 2 * SB_HEAD_DIM), 1) < SB_HEAD_DIM
    col = lax.broadcasted_iota(jnp.int32, (tk, 2 * tk), 1)
    key = jnp.where(col >= tk, col - tk, col)
    row = lax.broadcasted_iota(jnp.int32, (tk, 2 * tk), 0)
    causal_bias = jnp.where(key < row, 0.0, SB_MASKED)

    def block_diag(x):
        zero = jnp.zeros_like(x)
        return jnp.concatenate([jnp.where(head_a, x, zero), jnp.where(head_a, zero, x)], axis=0)

    def step(delta, diagonal):
        zs, v_bds = [], []
        for r in range(nd):
            jb = i * nd + r - delta
            start = pl.multiple_of(jnp.maximum(jb, 0) * tk, tk)
            k_bd = block_diag(k_ref[pl.ds(start, tk), :])
            v_bds.append(block_diag(v_ref[pl.ds(start, tk), :]))
            z = _dot_nt(q[r * tk:(r + 1) * tk, :], k_bd)
            first_key = jnp.where(jb == 0, FRONT - N_META, jnp.where(jb < 0, tk, 0))
            z = z + jnp.where(key[0:1, :] >= first_key, 0.0, SB_MASKED)
            if diagonal:
                z = z + causal_bias
            zs.append(z)
        z = jnp.concatenate(zs, axis=0)
        ws, sticks = [], []
        for lanes in (slice(0, tk), slice(tk, 2 * tk)):
            zh = z[:, lanes]
            s = jnp.maximum(zh, 0.0) + jnp.log2(1.0 + jnp.exp2(-jnp.abs(zh)))
            sums = _dot(s.astype(BF16), tt)
            carry = c_ref[:, lanes]
            ws.append(jnp.exp2(zh - s + sums[:, :tk] + carry).astype(BF16))
            carry = carry + sums[:, tk:]
            c_ref[:, lanes] = carry
            sticks.append(jnp.max(jnp.max(carry, axis=0, keepdims=True), axis=1, keepdims=True))
        w = jnp.concatenate(ws, axis=1)
        acc_ref[...] += jnp.concatenate(
            [_dot(w[r * tk:(r + 1) * tk, :], v_bds[r]) for r in range(nd)], axis=0)
        return jnp.maximum(sticks[0], sticks[1])[0, 0]

    acc_ref[...] = jnp.zeros_like(acc_ref)
    c_ref[...] = jnp.zeros_like(c_ref)
    step(0, True)
    stick = step(1, False)

    def live(state):
        delta, stick = state
        return jnp.logical_and(delta < (i + 1) * nd, stick > SB_LOG2_UNDERFLOW)

    lax.while_loop(live, lambda state: (state[0] + 1, step(state[0], False)), (2, stick))
    o_ref[...] = acc_ref[...].astype(o_ref.dtype)


def _stick_breaking(proj3, tq, tk):
    b, lp, _ = proj3.shape
    pair = 2 * SB_HEAD_DIM
    n_pairs = SB_WIDTH // pair
    kern = functools.partial(_sb_kernel, tq=tq, tk=tk)
    return pl.pallas_call(
        kern,
        grid=(b, n_pairs, lp // tq),
        in_specs=[
            pl.BlockSpec((None, tq, pair), lambda bi, hp, i: (bi, i, COL_SBQ // pair + hp)),
            pl.BlockSpec((None, lp, pair), lambda bi, hp, i: (bi, 0, COL_SBK // pair + hp)),
            pl.BlockSpec((None, lp, pair), lambda bi, hp, i: (bi, 0, COL_SBV // pair + hp)),
            _resident((tk, 2 * tk)),
        ],
        out_specs=pl.BlockSpec((None, tq, pair), lambda bi, hp, i: (bi, i, hp)),
        out_shape=jax.ShapeDtypeStruct((b, lp, SB_WIDTH), BF16),
        scratch_shapes=[
            pltpu.VMEM((tq, pair), F32),
            pltpu.VMEM((tq, 2 * tk), F32),
        ],
        compiler_params=pltpu.CompilerParams(
            dimension_semantics=("arbitrary", "arbitrary", "arbitrary"),
            vmem_limit_bytes=VMEM_LIMIT_BYTES),
        name="stick_breaking",
    )(proj3, proj3, proj3, _sb_tail_matrix(tk))


def _gla_constants(c, levels, mxu_levels):
    t = np.arange(c)[:, None]
    j = np.arange(c)[None, :]
    blocks = [(j <= t)]
    for l in range(mxu_levels):
        half = 1 << l
        p = (t >> (l + 1) << (l + 1)) + half
        upper = ((t >> l) & 1) == 1
        blocks.append(np.where(upper, (j >= p) & (j <= t), (j > t) & (j < p)))
    e = np.concatenate(blocks, axis=0).astype(np.float32)
    e2 = np.concatenate([e, e], axis=1)
    s = np.arange(c)[None, :]
    x = t ^ s
    lvl = np.where(x > 0, np.floor(np.log2(np.maximum(x, 1))).astype(np.int32), levels)
    lvl = np.where(s > t, -1, lvl).astype(np.int32)
    return jnp.asarray(e2, dtype=BF16), jnp.asarray(lvl)


def _gla_kernel(in_ref, wup_ref, bgk_ref, gn_ref, e2_ref, lvl_ref, o_ref, st_ref, *, c, n_chunks,
                levels, mxu_levels):
    dk, dv = GLA_HEAD_DK, GLA_HEAD_DV
    heads = [slice(h * dk, (h + 1) * dk) for h in range(GLA_HEADS)]
    q_cols = slice(COL_GQ - COL_GQ, COL_GK - COL_GQ)
    k_cols = slice(COL_GK - COL_GQ, COL_GV - COL_GQ)
    r_cols = slice(COL_GR - COL_GQ, COL_LR - COL_GQ)
    lr_cols = slice(COL_LR - COL_GQ, COL_LR - COL_GQ + LANES)

    @pl.when(pl.program_id(1) == 0)
    def _():
        st_ref[...] = jnp.zeros_like(st_ref)

    rank_lanes = lax.broadcasted_iota(jnp.int32, (c, LANES), 1) < GATE_RANK
    pos = lax.broadcasted_iota(jnp.int32, (c, GLA_K_WIDTH), 0)
    row_bit = [((pos >> l) & 1) == 1 for l in range(mxu_levels)]

    def chunk(rows):
        lr = jnp.where(rank_lanes, in_ref[rows, lr_cols], jnp.zeros((c, LANES), BF16))
        pre = _dot(lr, wup_ref[...]) + bgk_ref[...]
        g = (jnp.minimum(pre, 0.0) - jnp.log2(1.0 + jnp.exp2(-jnp.abs(pre)))) * (1.0 / GATE_TAU)
        g_hi, g_lo = _split_hi_lo(g)
        x = _dot(e2_ref[...], jnp.concatenate([g_hi, g_lo], axis=0))
        cum = x[:c]

        def level_exponent(l):
            if l < mxu_levels:
                return x[(l + 1) * c:(l + 2) * c]
            half = 1 << l
            parts = []
            for base in range(0, c, 2 * half):
                p = base + half
                ref = jnp.broadcast_to(cum[p - 1:p, :], (half, cum.shape[1]))
                parts.append(ref - cum[base:p])
                parts.append(cum[p:p + half] - ref)
            return jnp.concatenate(parts, axis=0)

        q = in_ref[rows, q_cols].astype(F32)
        k = in_ref[rows, k_cols].astype(F32)
        lvl = lvl_ref[...]

        def query_or_key(l):
            half = 1 << l
            if half >= SUBLANES:
                return jnp.concatenate(
                    [(q if (r0 // half) % 2 else k)[r0:r0 + half] for r0 in range(0, c, half)],
                    axis=0)
            return jnp.where(row_bit[l], q, k)

        q16, k16 = q.astype(BF16), k.astype(BF16)
        att = [jnp.where(lvl == levels, _dot_nt(q16[:, hs], k16[:, hs]), 0.0) for hs in heads]
        for l in range(levels):
            qk = (query_or_key(l) * jnp.exp2(level_exponent(l))).astype(BF16)
            att = [jnp.where(lvl == l, _dot_nt(qk[:, hs], qk[:, hs]), a)
                   for hs, a in zip(heads, att)]

        last = cum[c - 1:c, :]
        q_dec = (q * jnp.exp2(cum)).astype(BF16)
        k_dec = (k * jnp.exp2(last - cum)).astype(BF16)
        st_decay = jnp.exp2(last)
        outs = []
        for h, hs in enumerate(heads):
            v = in_ref[rows, COL_GV - COL_GQ + h * dv:COL_GV - COL_GQ + (h + 1) * dv]
            st = st_ref[h]
            o = _dot(att[h].astype(BF16), v) + _dot_nt(q_dec[:, hs], st.astype(BF16))
            st_ref[h] = st * st_decay[:, hs] + lax.dot_general(
                v, k_dec[:, hs], (((0,), (0,)), ((), ())), preferred_element_type=F32)
            outs.append(_rms(o, gn_ref[...]))
        r = in_ref[rows, r_cols].astype(F32)
        o_ref[rows, :] = (jnp.concatenate(outs, axis=1) * (r * jax.nn.sigmoid(r))).astype(o_ref.dtype)

    for n in range(n_chunks):
        chunk(slice(n * c, (n + 1) * c))


def _gla(proj3, w_up_pad, b_gk, head_gain, tm):
    b, lp, _ = proj3.shape
    c, levels, mxu_levels = GLA_CHUNK, GLA_LEVELS, GLA_MXU_LEVELS
    e2, lvl = _gla_constants(c, levels, mxu_levels)
    kw, vw = GLA_K_WIDTH, GLA_V_WIDTH
    n_chunks = tm // c
    kern = functools.partial(_gla_kernel, c=c, n_chunks=n_chunks, levels=levels,
                             mxu_levels=mxu_levels)
    return pl.pallas_call(
        kern,
        grid=(b, lp // tm),
        in_specs=[
            pl.BlockSpec(
                (pl.Squeezed(), pl.Element(tm), pl.Element(PROJ_WIDTH - COL_GQ)),
                lambda bi, t: (bi, pl.multiple_of(t * tm, Q_BLOCK), COL_GQ)),
            _resident((LANES, kw)),
            _resident((1, kw)),
            _resident((1, GLA_HEAD_DV)),
            _resident(((mxu_levels + 1) * c, 2 * c)),
            _resident((c, c)),
        ],
        out_specs=pl.BlockSpec((None, tm, vw), lambda bi, t: (bi, t, 0)),
        out_shape=jax.ShapeDtypeStruct((b, lp, vw), BF16),
        scratch_shapes=[pltpu.VMEM((GLA_HEADS, GLA_HEAD_DV, GLA_HEAD_DK), F32)],
        compiler_params=pltpu.CompilerParams(
            dimension_semantics=("arbitrary", "arbitrary"),
            vmem_limit_bytes=VMEM_LIMIT_BYTES),
        name="gla",
    )(proj3, w_up_pad, b_gk, head_gain, e2, lvl)


def _merge_kernel(x_ref, meta_ref, gpre_ref, wm_ref, sb_ref, og_ref, wsb_ref, wgla_ref, wo_ref,
                  gpost_ref, up_ref, gate_ref, down_ref, o_ref, up16_ref, gate16_ref, down16_ref,
                  h_ref, wm16_ref, *, tm):
    up16_ref[...] = up_ref[...].astype(BF16)
    gate16_ref[...] = gate_ref[...].astype(BF16)
    down16_ref[...] = down_ref[...].astype(BF16)

    @pl.when(jnp.logical_and(pl.program_id(0) == 0, pl.program_id(1) == 0))
    def _():
        for c0 in range(0, 2 * D_MODEL, MXU_WIDTH):
            cols = slice(c0, c0 + MXU_WIDTH)
            wm16_ref[:, cols] = wm_ref[cols, :].T.astype(BF16)

    h = _padded_rows(x_ref, meta_ref, h_ref, tm)
    hn = _rms(h, gpre_ref[...]).astype(BF16)
    m = _dot(hn, wm16_ref[...])
    a = _dot(sb_ref[...], wsb_ref[...])
    b = _dot(og_ref[...], wgla_ref[...])
    mixed = jax.nn.sigmoid(m[:, :D_MODEL]) * a + jax.nn.sigmoid(m[:, D_MODEL:]) * b
    mix = _dot(mixed.astype(BF16), wo_ref[...])
    o_ref[...] = h + _rms(mix, gpost_ref[...])


def _merge(x, meta, g_pre, w_in_t, sb, og, w_sb, w_gla, w_o, g_post, ffn_weights, layer, tm, lp):
    b = x.shape[0]
    n_tiles = lp // tm
    kern = functools.partial(_merge_kernel, tm=tm)
    merge_rows = pl.BlockSpec(
        (pl.Squeezed(), pl.Element(2 * D_MODEL), pl.Element(D_MODEL)),
        lambda bi, i: (0, COL_MERGE, 0), pipeline_mode=pl.Buffered(1))

    n_blocks = min(FFN_CAST_BLOCKS, (b * n_tiles) // len(ffn_weights))
    assert n_blocks >= 1
    cast_in, cast_out, cast_shapes = [], [], []
    for n, w in enumerate(ffn_weights):
        _, rows, cols = w.shape
        blk = rows // n_blocks

        def block(bi, i, n=n):
            return jnp.clip(bi * n_tiles + i - n * n_blocks, 0, n_blocks - 1)

        cast_in.append(pl.BlockSpec((None, blk, cols), lambda bi, i, f=block: (layer, f(bi, i), 0)))
        cast_out.append(pl.BlockSpec((blk, cols), lambda bi, i, f=block: (f(bi, i), 0)))
        cast_shapes.append(jax.ShapeDtypeStruct((rows, cols), BF16))

    return pl.pallas_call(
        kern,
        grid=(b, n_tiles),
        in_specs=[
            _x_window(tm),
            _resident((N_META, D_MODEL)),
            _resident((1, D_MODEL)),
            merge_rows,
            pl.BlockSpec((None, tm, SB_WIDTH), lambda bi, i: (bi, i, 0)),
            pl.BlockSpec((None, tm, GLA_V_WIDTH), lambda bi, i: (bi, i, 0)),
            _resident((SB_WIDTH, D_MODEL)),
            _resident((GLA_V_WIDTH, D_MODEL)),
            _resident((D_MODEL, D_MODEL)),
            _resident((1, D_MODEL)),
            *cast_in,
        ],
        out_specs=[pl.BlockSpec((None, tm, D_MODEL), lambda bi, i: (bi, i, 0)), *cast_out],
        out_shape=[jax.ShapeDtypeStruct((b, lp, D_MODEL), F32), *cast_shapes],
        scratch_shapes=[
            pltpu.VMEM((tm, D_MODEL), F32),
            pltpu.VMEM((D_MODEL, 2 * D_MODEL), BF16),
        ],
        compiler_params=pltpu.CompilerParams(
            dimension_semantics=("arbitrary", "arbitrary"), vmem_limit_bytes=VMEM_LIMIT_BYTES),
        name="merge",
    )(x, meta, g_pre, w_in_t, sb, og, w_sb, w_gla, w_o, g_post, *ffn_weights)


def _ffn_kernel(h_ref, hist_ref, gpre_ref, wup_ref, wgate_ref, cw_ref, cb_ref, wdown_ref,
                gpost_ref, o_ref, tail_ref, act_ref, *, tm):
    chunks = [slice(c0, c0 + FF_CHUNK) for c0 in range(0, D_FF, FF_CHUNK)]

    @pl.when(pl.program_id(1) == 0)
    def _():
        hist = _rms(hist_ref[...], gpre_ref[...]).astype(BF16)
        for cols in chunks:
            tail_ref[:, cols] = _dot(hist, wup_ref[:, cols])

    h = h_ref[...]
    hn = _rms(h, gpre_ref[...]).astype(BF16)
    row = lax.broadcasted_iota(jnp.int32, (SUBLANES, FF_CHUNK), 0)
    for cols in chunks:
        up = _dot(hn, wup_ref[:, cols])
        gate = _dot(hn, wgate_ref[:, cols])
        prev = tail_ref[:, cols]
        r1 = pltpu.roll(up, 1, 0)
        r2 = pltpu.roll(up, 2, 0)
        last, before_last = prev[SUBLANES - 1:SUBLANES, :], prev[SUBLANES - 2:SUBLANES - 1, :]
        head1 = jnp.where(row == 0, last, r1[:SUBLANES])
        head2 = jnp.where(row == 0, before_last, jnp.where(row == 1, last, r2[:SUBLANES]))
        up1 = jnp.concatenate([head1, r1[SUBLANES:]], axis=0)
        up2 = jnp.concatenate([head2, r2[SUBLANES:]], axis=0)
        tail_ref[:, cols] = up[tm - SUBLANES:, :]
        cw = cw_ref[:, cols]
        y = cw[0:1, :] * up2 + cw[1:2, :] * up1 + cw[2:3, :] * up + cb_ref[:, cols]
        act_ref[:, cols] = (jax.nn.gelu(y, approximate=True) * gate).astype(BF16)
    ffn = _dot(act_ref[...], wdown_ref[...])
    o_ref[...] = h + _rms(ffn, gpost_ref[...])


def _ffn(h3, g_pre, w_up, w_gate, conv_w, conv_b, w_down, g_post, tm):
    b, lp, _ = h3.shape
    s = lp - FRONT
    kern = functools.partial(_ffn_kernel, tm=tm)

    def rows(n, start):
        return pl.BlockSpec((pl.Squeezed(), pl.Element(n), pl.Element(D_MODEL)), start)

    return pl.pallas_call(
        kern,
        grid=(b, s // tm),
        in_specs=[
            rows(tm, lambda bi, i: (bi, pl.multiple_of(FRONT + i * tm, Q_BLOCK), 0)),
            rows(SUBLANES, lambda bi, i: (bi, FRONT - SUBLANES, 0)),
            _resident((1, D_MODEL)),
            _resident((D_MODEL, D_FF)),
            _resident((D_MODEL, D_FF)),
            _resident((CONV_W, D_FF)),
            _resident((1, D_FF)),
            _resident((D_FF, D_MODEL)),
            _resident((1, D_MODEL)),
        ],
        out_specs=pl.BlockSpec((None, tm, D_MODEL), lambda bi, i: (bi, i, 0)),
        out_shape=jax.ShapeDtypeStruct((b, s, D_MODEL), F32),
        scratch_shapes=[
            pltpu.VMEM((SUBLANES, D_FF), F32),
            pltpu.VMEM((tm, D_FF), BF16),
        ],
        compiler_params=pltpu.CompilerParams(
            dimension_semantics=("arbitrary", "arbitrary"), vmem_limit_bytes=VMEM_LIMIT_BYTES),
        name="ffn",
    )(h3, h3, g_pre, w_up, w_gate, conv_w, conv_b, w_down, g_post)


def _pick_tile(n, want):
    t = want
    while n % t:
        t -= Q_BLOCK
    return t


def kernel(x, meta_tokens, norm_mix_pre, w_in, w_gk_up, b_gk, gla_head_norm, w_sb_out, w_gla_out, w_o,
           norm_mix_post, norm_ffn_pre, w_ffn_up, w_ffn_gate, conv_w, conv_b, w_ffn_down, norm_ffn_post):
    b, s, _ = x.shape
    assert s % Q_BLOCK == 0 and norm_mix_pre.shape[0] == 1
    lp = FRONT + s
    tm = _pick_tile(lp, ROW_TILE)
    tq = _pick_tile(lp, SB_TQ)
    tf = _pick_tile(s, FFN_TILE)
    d = 0

    w_in_t = jnp.swapaxes(w_in, 1, 2)
    w_up_pad = jnp.pad((w_gk_up[d] * LOG2_E).astype(BF16), ((0, LANES - GATE_RANK), (0, 0)))
    b_gk2 = b_gk[d][None, :] * LOG2_E

    proj3 = _inproj(x, meta_tokens, norm_mix_pre[d][None, :], w_in_t, tm, lp)
    sb = _stick_breaking(proj3, tq, SB_TK)
    og = _gla(proj3, w_up_pad, b_gk2, gla_head_norm[d][None, :], tm)
    h1, w_up16, w_gate16, w_down16 = _merge(
        x, meta_tokens, norm_mix_pre[d][None, :], w_in_t, sb, og,
        w_sb_out[d].astype(BF16), w_gla_out[d].astype(BF16), w_o[d].astype(BF16),
        norm_mix_post[d][None, :], (w_ffn_up, w_ffn_gate, w_ffn_down), d, tm, lp)
    return _ffn(h1, norm_ffn_pre[d][None, :], w_up16, w_gate16, conv_w[d], conv_b[d][None, :],
                w_down16, norm_ffn_post[d][None, :], tf)
```
